```python
import math
import jax, jax.numpy as jnp
from jax import lax
import numpy as np

D_MODEL = 1024
BATCH = 8
SEQ = 4096
DEPTH = 2

HEAD_DIM = 64
A_HEADS = 4
A_VDIM = 2 * HEAD_DIM
B_HEADS = 8
B_BRANCHES = ((128, 1), (512, 4), (2048, 16))
B_BLOCK = 128
C_HEADS = 16
C_Q_RANK = 384
C_KV_RANK = 256
C_NOPE = 64
C_ROPE = 32
C_VDIM = 64
C_SCALE = (C_NOPE + C_ROPE) ** -0.5
IDX_HEADS = 8
IDX_DIM = 64
IDX_ROPE = 32
TOPK_MAX = 256
Q_BLOCK = 128
D_FF = 4 * D_MODEL
PLE_DIM = 256
ROPE_THETA = 10000.0
EPS = 1e-6
N_EVEN = (DEPTH + 1) // 2
N_ODD = DEPTH // 2
AB_SPLITS = (A_HEADS * 2 * HEAD_DIM, A_HEADS * 2 * HEAD_DIM, A_HEADS * A_VDIM,
             B_HEADS * HEAD_DIM, B_HEADS * HEAD_DIM, B_HEADS * HEAD_DIM)
AB_IN = sum(AB_SPLITS)
AB_OUT = A_HEADS * A_VDIM + B_HEADS * HEAD_DIM
C_SPLITS = (C_Q_RANK, C_KV_RANK, C_ROPE, IDX_DIM, IDX_HEADS)
C_IN = sum(C_SPLITS)
C_OUT = C_HEADS * C_VDIM

kernel_name = "hybrid_diff_dilated_dsa_trunk"

F32 = jnp.float32


def rms(x, g):
    xf = x.astype(F32)
    y = xf * lax.rsqrt(jnp.mean(xf * xf, axis=-1, keepdims=True) + EPS) * g.astype(F32)
    return y.astype(x.dtype)


def rope(x, pos):
    half = x.shape[-1] // 2
    inv = ROPE_THETA ** (-jnp.arange(half, dtype=F32) / half)
    ang = pos.astype(F32)[:, None] * inv[None, :]
    cos = jnp.cos(ang)[:, None, :]
    sin = jnp.sin(ang)[:, None, :]
    x1 = x[..., :half].astype(F32)
    x2 = x[..., half:].astype(F32)
    return jnp.concatenate([x1 * cos - x2 * sin, x1 * sin + x2 * cos], axis=-1).astype(x.dtype)


def split_cols(t, sizes):
    offs = np.cumsum(sizes)[:-1].tolist()
    return jnp.split(t, offs, axis=-1)


def diff_attention(q, k, v, lam):
    S = q.shape[3]
    scale = HEAD_DIM ** -0.5
    outs = []
    for start in range(0, S, Q_BLOCK):
        end = start + Q_BLOCK
        s = jnp.einsum('bhmqd,bhmkd->bhmqk', q[:, :, :, start:end], k[:, :, :, :end]).astype(F32) * scale
        causal = jnp.arange(start, end)[:, None] >= jnp.arange(end)[None, :]
        pr = jax.nn.softmax(jnp.where(causal, s, -jnp.inf), axis=-1)
        a = pr[:, :, 0] - lam * pr[:, :, 1]
        outs.append(jnp.einsum('bhqk,bhkd->bhqd', a.astype(v.dtype), v[:, :, :end]))
    return jnp.concatenate(outs, axis=2)


def dilated_branch(q, k, v, window, dilation):
    Bn, H, S, dh = q.shape
    steps = window // dilation
    L = S // dilation
    nb = -(-L // B_BLOCK)
    Lp = nb * B_BLOCK

    def strided(t):
        t = t.reshape(Bn, H, L, dilation, dh).transpose(0, 1, 3, 2, 4)
        t = jnp.pad(t, ((0, 0), (0, 0), (0, 0), (0, Lp - L), (0, 0)))
        return t.reshape(Bn, H, dilation, nb, B_BLOCK, dh)

    def with_prev(t):
        prev = jnp.pad(t, ((0, 0), (0, 0), (0, 0), (1, 0), (0, 0), (0, 0)))[:, :, :, :-1]
        return jnp.concatenate([prev, t], axis=-2)

    qb = strided(q)
    kk = with_prev(strided(k))
    vv = with_prev(strided(v))
    s = jnp.einsum('bhcnqd,bhcnkd->bhcnqk', qb, kk).astype(F32) * dh ** -0.5
    ki = jnp.arange(2 * B_BLOCK)[None, :]
    dist = (jnp.arange(B_BLOCK)[:, None] + B_BLOCK) - ki
    blk = jnp.arange(nb)[:, None, None]
    ok = (dist >= 0) & (dist <= steps) & ((blk > 0) | (ki >= B_BLOCK))
    s = jnp.where(ok, s, -jnp.inf)
    lse = jax.nn.logsumexp(s, axis=-1)
    pr = jnp.exp(s - lse[..., None])
    o = jnp.einsum('bhcnqk,bhcnkd->bhcnqd', pr.astype(v.dtype), vv)
    o = o.reshape(Bn, H, dilation, Lp, dh)[:, :, :, :L].transpose(0, 1, 3, 2, 4).reshape(Bn, H, S, dh)
    lse = lse.reshape(Bn, H, dilation, Lp)[..., :L].transpose(0, 1, 3, 2).reshape(Bn, H, S)
    return o, lse


def dilated_attention(q, k, v):
    res = [dilated_branch(q, k, v, w, d) for (w, d) in B_BRANCHES]
    outs = jnp.stack([r[0] for r in res]).astype(F32)
    lses = jnp.stack([r[1] for r in res])
    wts = jax.nn.softmax(lses, axis=0)
    return jnp.einsum('gbhs,gbhsd->bhsd', wts, outs)


def mixer_ab(h, pos, w_in, w_out, lq1, lk1, lq2, lk2, g_sub, lam_init):
    Bn, S, _ = h.shape
    aq, ak, av, bq, bk, bv = split_cols(h @ w_in, AB_SPLITS)
    def a_qk(t):
        t = rope(t.reshape(Bn, S, A_HEADS * 2, HEAD_DIM), pos)
        return t.reshape(Bn, S, A_HEADS, 2, HEAD_DIM).transpose(0, 2, 3, 1, 4)
    av = av.reshape(Bn, S, A_HEADS, A_VDIM).transpose(0, 2, 1, 3)
    lam = (jnp.exp(jnp.sum(lq1.astype(F32) * lk1.astype(F32)))
           - jnp.exp(jnp.sum(lq2.astype(F32) * lk2.astype(F32))) + lam_init)
    ao = diff_attention(a_qk(aq), a_qk(ak), av, lam)
    ao = rms(ao, g_sub) * (1.0 - lam_init)
    ao = ao.transpose(0, 2, 1, 3).reshape(Bn, S, A_HEADS * A_VDIM)
    def b_heads(t, rot):
        t = t.reshape(Bn, S, B_HEADS, HEAD_DIM)
        t = rope(t, pos) if rot else t
        return t.transpose(0, 2, 1, 3)
    bo = dilated_attention(b_heads(bq, True), b_heads(bk, True), b_heads(bv, False))
    bo = bo.transpose(0, 2, 1, 3).reshape(Bn, S, B_HEADS * HEAD_DIM).astype(h.dtype)
    return jnp.concatenate([ao.astype(h.dtype), bo], axis=-1) @ w_out


def partial_rope(t, pos):
    return jnp.concatenate([rope(t[..., :IDX_ROPE], pos), t[..., IDX_ROPE:]], axis=-1)


def mixer_c(h, pos, w_in, g_cq, g_ckv, w_uq, w_qi, w_uk, w_uv, w_out):
    Bn, S, _ = h.shape
    cq, ckv, krope, kidx, widx = split_cols(h @ w_in, C_SPLITS)
    cq = rms(cq, g_cq)
    ckv = rms(ckv, g_ckv)
    q = (cq @ w_uq).reshape(Bn, S, C_HEADS, C_NOPE + C_ROPE)
    q_lat = jnp.einsum('bshd,chd->bshc', q[..., :C_NOPE], w_uk)
    q_full = jnp.concatenate([q_lat, rope(q[..., C_NOPE:], pos)], axis=-1)
    k_rope = rope(krope[:, :, None, :], pos)[:, :, 0]
    kv = jnp.concatenate([ckv, k_rope], axis=-1)
    qi = partial_rope((cq @ w_qi).reshape(Bn, S, IDX_HEADS, IDX_DIM), pos)
    ki = partial_rope(kidx[:, :, None, :], pos)[:, :, 0]
    wi = widx.astype(F32) * IDX_HEADS ** -0.5
    k_sel = min(TOPK_MAX, S // 4)
    nb = S // Q_BLOCK

    def blocks(t):
        return t.reshape((Bn, nb, Q_BLOCK) + t.shape[2:]).swapaxes(0, 1)

    def one_block(args):
        qf, qib, wib, start = args
        tq = start + jnp.arange(Q_BLOCK)
        dots = jnp.einsum('bqhd,bsd->bqhs', qib, ki).astype(F32)
        score = jnp.einsum('bqh,bqhs->bqs', wib, jax.nn.relu(dots)) * IDX_DIM ** -0.5
        causal = jnp.arange(S)[None, :] <= tq[:, None]
        score = jnp.where(causal, score, -jnp.inf)
        _, idx = lax.top_k(score, k_sel)
        valid = idx <= tq[None, :, None]
        sel = jax.vmap(lambda kvb, ib: kvb[ib])(kv, idx)
        s = jnp.einsum('bqhc,bqkc->bqhk', qf, sel).astype(F32) * C_SCALE
        pr = jax.nn.softmax(jnp.where(valid[:, :, None, :], s, -jnp.inf), axis=-1)
        o_lat = jnp.einsum('bqhk,bqkc->bqhc', pr.astype(sel.dtype), sel[..., :C_KV_RANK])
        return jnp.einsum('bqhc,chd->bqhd', o_lat, w_uv)

    o = lax.map(one_block, (blocks(q_full), blocks(qi), blocks(wi), jnp.arange(nb) * Q_BLOCK))
    o = o.swapaxes(0, 1).reshape(Bn, S, C_OUT)
    return o @ w_out


def setup_inputs(seed: int = 0) -> dict:
    key = jax.random.key(seed)
    ks = iter(jax.random.split(key, 32))

    def nrm(shape, fan_in):
        return jax.random.normal(next(ks), shape, F32) * fan_in ** -0.5

    def gain(shape):
        return 1.0 + 0.02 * jax.random.normal(next(ks), shape, F32)

    def small(shape):
        return 0.1 * jax.random.normal(next(ks), shape, F32)

    return {
        "x": jax.random.normal(next(ks), (BATCH, SEQ, D_MODEL), F32),
        "p": jax.random.normal(next(ks), (DEPTH, BATCH, SEQ, PLE_DIM), F32),
        "g_mix_pre": gain((DEPTH, D_MODEL)),
        "g_mix_post": gain((DEPTH, D_MODEL)),
        "g_mlp_pre": gain((DEPTH, D_MODEL)),
        "g_mlp_post": gain((DEPTH, D_MODEL)),
        "w_mlp_in": nrm((DEPTH, D_MODEL, D_FF), D_MODEL),
        "w_mlp_out": nrm((DEPTH, D_FF, D_MODEL), D_FF),
        "w_ple_proj": nrm((DEPTH, PLE_DIM, D_MODEL), PLE_DIM),
        "w_ple_gate": nrm((DEPTH, D_MODEL, D_MODEL), D_MODEL),
        "w_in_ab": nrm((N_EVEN, D_MODEL, AB_IN), D_MODEL),
        "w_out_ab": nrm((N_EVEN, AB_OUT, D_MODEL), AB_OUT),
        "diff_lq1": small((N_EVEN, HEAD_DIM)),
        "diff_lk1": small((N_EVEN, HEAD_DIM)),
        "diff_lq2": small((N_EVEN, HEAD_DIM)),
        "diff_lk2": small((N_EVEN, HEAD_DIM)),
        "g_diff_sub": gain((N_EVEN, A_VDIM)),
        "w_in_c": nrm((N_ODD, D_MODEL, C_IN), D_MODEL),
        "g_cq": gain((N_ODD, C_Q_RANK)),
        "g_ckv": gain((N_ODD, C_KV_RANK)),
        "w_uq": nrm((N_ODD, C_Q_RANK, C_HEADS * (C_NOPE + C_ROPE)), C_Q_RANK),
        "w_qi": nrm((N_ODD, C_Q_RANK, IDX_HEADS * IDX_DIM), C_Q_RANK),
        "w_uk": nrm((N_ODD, C_KV_RANK, C_HEADS, C_NOPE), C_KV_RANK),
        "w_uv": nrm((N_ODD, C_KV_RANK, C_HEADS, C_VDIM), C_KV_RANK),
        "w_out_c": nrm((N_ODD, C_OUT, D_MODEL), C_OUT),
    }


def reference(x, p, g_mix_pre, g_mix_post, g_mlp_pre, g_mlp_post, w_mlp_in, w_mlp_out,
              w_ple_proj, w_ple_gate, w_in_ab, w_out_ab, diff_lq1, diff_lk1, diff_lq2,
              diff_lk2, g_diff_sub, w_in_c, g_cq, g_ckv, w_uq, w_qi, w_uk, w_uv, w_out_c):
    S = x.shape[1]
    pos = jnp.arange(S)
    h = x
    for i in range(DEPTH):
        hn = rms(h, g_mix_pre[i])
        j = i // 2
        if i % 2 == 0:
            lam_init = 0.8 - 0.6 * math.exp(-0.3 * i)
            y = mixer_ab(hn, pos, w_in_ab[j], w_out_ab[j], diff_lq1[j], diff_lk1[j],
                         diff_lq2[j], diff_lk2[j], g_diff_sub[j], lam_init)
        else:
            y = mixer_c(hn, pos, w_in_c[j], g_cq[j], g_ckv[j], w_uq[j], w_qi[j],
                        w_uk[j], w_uv[j], w_out_c[j])
        h = h + rms(y, g_mix_post[i])
        hn = rms(h, g_mlp_pre[i])
        y = jnp.square(jax.nn.relu(hn @ w_mlp_in[i])) @ w_mlp_out[i]
        h = h + rms(y, g_mlp_post[i])
        gate = jax.nn.sigmoid(h @ w_ple_gate[i])
        h = h + gate * (p[i] @ w_ple_proj[i])
    return h
```

```python
import functools
import math

import jax
import jax.numpy as jnp
from jax import lax
from jax.experimental import pallas as pl
from jax.experimental.pallas import tpu as pltpu

F32 = jnp.float32
BF16 = jnp.bfloat16
I32 = jnp.int32

LANES = 128
VMEM_LIMIT = 56 * 1024 * 1024

D_MODEL = 1024
HEAD_DIM = 64
A_HEADS = 4
A_VDIM = 2 * HEAD_DIM
B_HEADS = 8
B_BRANCHES = ((128, 1), (512, 4), (2048, 16))
C_HEADS = 16
C_Q_RANK = 384
C_KV_RANK = 256
C_NOPE = 64
C_ROPE = 32
C_VDIM = 64
C_SCALE = (C_NOPE + C_ROPE) ** -0.5
IDX_HEADS = 8
IDX_DIM = 64
IDX_ROPE = 32
TOPK_MAX = 256
D_FF = 4 * D_MODEL
PLE_DIM = 256
ROPE_THETA = 10000.0
EPS = 1e-6
NEG = -1e30
INT_MIN = -2 ** 31

C_PAD = 1024
C_KV_W = C_KV_RANK + LANES
DSA_Q = 128
DSA_KC = 512
ATT_T = 256


def _cparams(sem):
    return pltpu.CompilerParams(dimension_semantics=sem, vmem_limit_bytes=VMEM_LIMIT)


def _rms_f32(x, g):
    return x * lax.rsqrt(jnp.mean(x * x, axis=-1, keepdims=True) + EPS) * g


def _rope_tile(y, cos, sin, half):
    lane = lax.broadcasted_iota(I32, y.shape, 1)
    first = (lane % (2 * half)) < half
    partner = jnp.where(first, pltpu.roll(y, LANES - half, 1), pltpu.roll(y, half, 1))
    return y * cos + partner * sin


def _rope_tables(seq, half, period, rot):
    inv = ROPE_THETA ** (-jnp.arange(half, dtype=F32) / half)
    ang = jnp.arange(seq).astype(F32)[:, None] * inv[None, :]
    lane = jnp.arange(LANES)
    g = lane % period
    idx = g % half
    on = (g < rot)[None, :]
    cos = jnp.where(on, jnp.cos(ang)[:, idx], 1.0)
    sgn = jnp.where(g % (2 * half) < half, -1.0, 1.0)[None, :]
    sin = jnp.where(on, jnp.sin(ang)[:, idx] * sgn, 0.0)
    return cos.astype(F32), sin.astype(F32)


def _rms_proj_kernel(x_ref, g_ref, w_ref, cos_ref, sin_ref, o_ref, *, rope_groups, half):
    xn = _rms_f32(x_ref[...], g_ref[...]).astype(BF16)
    n = w_ref.shape[1]
    slab = 512
    for s0 in range(0, n, slab):
        y = jnp.dot(xn, w_ref[:, s0:s0 + slab], preferred_element_type=F32)
        for c0 in range(0, slab, LANES):
            yc = y[:, c0:c0 + LANES]
            if (s0 + c0) // LANES in rope_groups:
                yc = _rope_tile(yc, cos_ref[...], sin_ref[...], half)
            o_ref[:, s0 + c0:s0 + c0 + LANES] = yc.astype(o_ref.dtype)


def _rms_proj(x, g, w, cos, sin, rope_groups, half, out_dtype, tm=512):
    bsz, seq, d = x.shape
    n = w.shape[1]
    return pl.pallas_call(
        functools.partial(_rms_proj_kernel, rope_groups=frozenset(rope_groups), half=half),
        grid=(bsz, seq // tm),
        in_specs=[
            pl.BlockSpec((None, tm, d), lambda b, i: (b, i, 0)),
            pl.BlockSpec((1, d), lambda b, i: (0, 0)),
            pl.BlockSpec((d, n), lambda b, i: (0, 0)),
            pl.BlockSpec((tm, LANES), lambda b, i: (i, 0)),
            pl.BlockSpec((tm, LANES), lambda b, i: (i, 0)),
        ],
        out_specs=pl.BlockSpec((None, tm, n), lambda b, i: (b, i, 0)),
        out_shape=jax.ShapeDtypeStruct((bsz, seq, n), out_dtype),
        compiler_params=_cparams(("parallel", "parallel")),
        name="rms_proj",
    )(x, g.reshape(1, d), w, cos, sin)


def _outproj_kernel(*refs, n_in):
    y_refs = refs[:n_in]
    w_ref, g_ref, h_ref, o_ref = refs[n_in:]
    acc = None
    off = 0
    for y_ref in y_refs:
        k = y_ref.shape[1]
        part = jnp.dot(y_ref[...], w_ref[off:off + k, :], preferred_element_type=F32)
        acc = part if acc is None else acc + part
        off += k
    o_ref[...] = h_ref[...] + _rms_f32(acc, g_ref[...])


def _outproj_res(ys, w, g, h, tm=512):
    bsz, seq, d = h.shape
    in_specs = [pl.BlockSpec((None, tm, y.shape[2]), lambda b, i: (b, i, 0)) for y in ys]
    in_specs += [
        pl.BlockSpec(w.shape, lambda b, i: (0, 0)),
        pl.BlockSpec((1, d), lambda b, i: (0, 0)),
        pl.BlockSpec((None, tm, d), lambda b, i: (b, i, 0)),
    ]
    return pl.pallas_call(
        functools.partial(_outproj_kernel, n_in=len(ys)),
        grid=(bsz, seq // tm),
        in_specs=in_specs,
        out_specs=pl.BlockSpec((None, tm, d), lambda b, i: (b, i, 0)),
        out_shape=jax.ShapeDtypeStruct(h.shape, F32),
        compiler_params=_cparams(("parallel", "parallel")),
        name="outproj_res",
    )(*ys, w, g.reshape(1, d), h)


def _mlp_kernel(h_ref, g1_ref, w1_ref, w2_ref, g2_ref, o_ref, xn_ref, acc_ref):
    j = pl.program_id(1)

    @pl.when(j == 0)
    def _():
        xn_ref[...] = _rms_f32(h_ref[...], g1_ref[...]).astype(BF16)
        acc_ref[...] = jnp.zeros_like(acc_ref)

    a = jnp.dot(xn_ref[...], w1_ref[...], preferred_element_type=F32)
    a = jnp.square(jnp.maximum(a, 0.0)).astype(BF16)
    acc_ref[...] += jnp.dot(a, w2_ref[...], preferred_element_type=F32)

    @pl.when(j == pl.num_programs(1) - 1)
    def _():
        o_ref[...] = h_ref[...] + _rms_f32(acc_ref[...], g2_ref[...])


def _mlp(h, g1, w1, w2, g2, tm=1024, tf=512):
    t, d = h.shape
    ff = w1.shape[1]
    return pl.pallas_call(
        _mlp_kernel,
        grid=(t // tm, ff // tf),
        in_specs=[
            pl.BlockSpec((tm, d), lambda i, j: (i, 0)),
            pl.BlockSpec((1, d), lambda i, j: (0, 0)),
            pl.BlockSpec((d, tf), lambda i, j: (0, j)),
            pl.BlockSpec((tf, d), lambda i, j: (j, 0)),
            pl.BlockSpec((1, d), lambda i, j: (0, 0)),
        ],
        out_specs=pl.BlockSpec((tm, d), lambda i, j: (i, 0)),
        out_shape=jax.ShapeDtypeStruct((t, d), F32),
        scratch_shapes=[pltpu.VMEM((tm, d), BF16), pltpu.VMEM((tm, d), F32)],
        compiler_params=_cparams(("parallel", "arbitrary")),
        name="mlp",
    )(h, g1.reshape(1, d), w1, w2, g2.reshape(1, d))


def _ple_kernel(h_ref, p_ref, wg_ref, wp_ref, o_ref):
    h = h_ref[...]
    gate = jax.nn.sigmoid(jnp.dot(h.astype(BF16), wg_ref[...], preferred_element_type=F32))
    emb = jnp.dot(p_ref[...].astype(BF16), wp_ref[...], preferred_element_type=F32)
    o_ref[...] = h + gate * emb


def _ple(h, p, wg, wp, tm=512):
    t, d = h.shape
    pd = p.shape[1]
    return pl.pallas_call(
        _ple_kernel,
        grid=(t // tm,),
        in_specs=[
            pl.BlockSpec((tm, d), lambda i: (i, 0)),
            pl.BlockSpec((tm, pd), lambda i: (i, 0)),
            pl.BlockSpec((d, d), lambda i: (0, 0)),
            pl.BlockSpec((pd, d), lambda i: (0, 0)),
        ],
        out_specs=pl.BlockSpec((tm, d), lambda i: (i, 0)),
        out_shape=jax.ShapeDtypeStruct((t, d), F32),
        compiler_params=_cparams(("parallel",)),
        name="ple",
    )(h, p, wg, wp)


def _pair_attn_kernel(q_ref, k_ref, v_ref, bias_ref, lam_ref, gsub_ref, o_ref,
                      m_ref, l_ref, acc_ref, *, windowed, diff, lam_init):
    i = pl.program_id(2)
    t = q_ref.shape[0]
    nb = bias_ref.shape[0]
    lane = lax.broadcasted_iota(I32, (t, LANES), 1)
    q = q_ref[...] * jnp.asarray(HEAD_DIM ** -0.5, BF16)
    zero = jnp.zeros_like(q)
    qs = jnp.concatenate([jnp.where(lane < HEAD_DIM, q, zero),
                          jnp.where(lane >= HEAD_DIM, q, zero)], axis=0)
    m_ref[...] = jnp.full_like(m_ref, NEG)
    l_ref[...] = jnp.zeros_like(l_ref)
    acc_ref[...] = jnp.zeros_like(acc_ref)

    def body(j, carry):
        start = pl.multiple_of(j * t, t)
        kj = k_ref[pl.ds(start, t), :]
        vj = v_ref[pl.ds(start, t), :]
        s = lax.dot_general(qs, kj, (((1,), (1,)), ((), ())), preferred_element_type=F32)
        bias = bias_ref[jnp.minimum(i - j, nb - 1)]
        s = (s.reshape(2, t, t) + bias[None]).reshape(2 * t, t)
        m_old = m_ref[...]
        m_new = jnp.maximum(m_old, jnp.max(s, axis=-1, keepdims=True))
        alpha = jnp.exp(m_old - m_new)
        p = jnp.exp(s - m_new)
        l_ref[...] = alpha * l_ref[...] + jnp.sum(p, axis=-1, keepdims=True)
        acc_ref[...] = alpha * acc_ref[...] + jnp.dot(p.astype(BF16), vj,
                                                      preferred_element_type=F32)
        m_ref[...] = m_new
        return carry

    lo = jnp.maximum(i - (nb - 1), 0) if windowed else 0
    lax.fori_loop(lo, i + 1, body, 0)

    o = acc_ref[...] / l_ref[...]
    oa, ob = o[:t], o[t:]
    if diff:
        lp = lam_ref[...]
        lam = (jnp.exp(jnp.sum(lp[0:1] * lp[1:2], axis=-1, keepdims=True))
               - jnp.exp(jnp.sum(lp[2:3] * lp[3:4], axis=-1, keepdims=True)) + lam_init)
        d = oa - lam * ob
        o_ref[...] = (_rms_f32(d, gsub_ref[...]) * (1.0 - lam_init)).astype(o_ref.dtype)
    else:
        o_ref[...] = jnp.where(lane < HEAD_DIM, oa, ob).astype(o_ref.dtype)


def _pair_attn(qkv, qcol, kcol, vcol, bias, lam_params, gsub, *, windowed, diff, lam_init):
    bsz, seq, _ = qkv.shape
    t = ATT_T
    nblk = 4
    kern = functools.partial(_pair_attn_kernel, windowed=windowed, diff=diff, lam_init=lam_init)
    return pl.pallas_call(
        kern,
        grid=(bsz, nblk, seq // t),
        in_specs=[
            pl.BlockSpec((None, t, LANES), lambda b, h, i: (b, i, qcol + h)),
            pl.BlockSpec((None, seq, LANES), lambda b, h, i: (b, 0, kcol + h)),
            pl.BlockSpec((None, seq, LANES), lambda b, h, i: (b, 0, vcol + h)),
            pl.BlockSpec(bias.shape, lambda b, h, i: (0, 0, 0)),
            pl.BlockSpec(lam_params.shape, lambda b, h, i: (0, 0)),
            pl.BlockSpec(gsub.shape, lambda b, h, i: (0, 0)),
        ],
        out_specs=pl.BlockSpec((None, t, LANES), lambda b, h, i: (b, i, h)),
        out_shape=jax.ShapeDtypeStruct((bsz, seq, nblk * LANES), BF16),
        scratch_shapes=[pltpu.VMEM((2 * t, 1), F32), pltpu.VMEM((2 * t, 1), F32),
                        pltpu.VMEM((2 * t, LANES), F32)],
        compiler_params=_cparams(("parallel", "parallel", "arbitrary")),
        name="diff_attn" if diff else "dilated_attn",
    )(qkv, qkv, qkv, bias, lam_params, gsub)


def _causal_bias(t):
    r = jnp.arange(t)
    tri = jnp.where(r[:, None] >= r[None, :], 0.0, NEG).astype(F32)
    return jnp.stack([tri, jnp.zeros((t, t), F32)])


def _dilated_bias(t):
    max_dist = max(w for w, _ in B_BRANCHES)
    n_off = max_dist // t + 1
    r = jnp.arange(t)
    tiles = []
    for o in range(n_off):
        dist = o * t + r[:, None] - r[None, :]
        cnt = jnp.zeros((t, t), F32)
        for w, d in B_BRANCHES:
            cnt = cnt + ((dist >= 0) & (dist <= w) & (dist % d == 0)).astype(F32)
        tiles.append(jnp.where(cnt > 0, jnp.log(jnp.maximum(cnt, 1.0)), NEG))
    return jnp.stack(tiles).astype(F32)


def _c_prep_kernel(pc_ref, gq_ref, gkv_ref, wn_ref, wr_ref, wqi_ref, wuk_ref,
                   cc_ref, sc_ref, cp_ref, sp_ref,
                   qf_ref, kv_ref, qi_ref, ki_ref, wi_ref):
    tm = pc_ref.shape[0]
    half = C_ROPE // 2
    cqn = _rms_f32(pc_ref[:, 0:C_Q_RANK], gq_ref[...]).astype(BF16)
    c0 = C_Q_RANK
    kv_ref[:, 0:C_KV_RANK] = _rms_f32(pc_ref[:, c0:c0 + C_KV_RANK], gkv_ref[...]).astype(BF16)
    c0 += C_KV_RANK
    kv_ref[:, C_KV_RANK:C_KV_W] = _rope_tile(pc_ref[:, c0:c0 + LANES], cc_ref[...], sc_ref[...],
                                             half).astype(BF16)
    c0 += LANES
    ki_ref[...] = _rope_tile(pc_ref[:, c0:c0 + LANES], cp_ref[...], sp_ref[...], half).astype(BF16)
    c0 += LANES
    wi_ref[...] = pc_ref[:, c0:c0 + LANES] * (IDX_HEADS ** -0.5)

    qi = jnp.dot(cqn, wqi_ref[...], preferred_element_type=F32)
    for c in range(0, IDX_HEADS * IDX_DIM, LANES):
        qi_ref[:, c:c + LANES] = _rope_tile(qi[:, c:c + LANES], cp_ref[...], sp_ref[...],
                                            half).astype(BF16)

    qn = jnp.dot(cqn, wn_ref[...], preferred_element_type=F32).astype(BF16)
    qr = jnp.dot(cqn, wr_ref[...], preferred_element_type=F32)
    for h in range(C_HEADS):
        pair = qn[:, (h // 2) * LANES:(h // 2 + 1) * LANES]
        lat = jnp.dot(pair, wuk_ref[h], preferred_element_type=F32) * C_SCALE
        rot = _rope_tile(qr[:, h * LANES:(h + 1) * LANES], cc_ref[...], sc_ref[...], half) * C_SCALE
        for blk in range(tm // DSA_Q):
            rows = slice(blk * DSA_Q, (blk + 1) * DSA_Q)
            qf_ref[blk, h, :, 0:C_KV_RANK] = lat[rows].astype(BF16)
            qf_ref[blk, h, :, C_KV_RANK:C_KV_W] = rot[rows].astype(BF16)


def _c_prep(pc, gq, gkv, wn, wr, wqi, wuk, cc, sc, cp, sp, tm=256):
    bsz, seq, _ = pc.shape
    nq = seq // DSA_Q
    bq = tm // DSA_Q
    row = lambda b, i: (b, i, 0)
    tab = pl.BlockSpec((tm, LANES), lambda b, i: (i, 0))
    full2 = lambda a: pl.BlockSpec(a.shape, lambda b, i: (0, 0))
    return pl.pallas_call(
        _c_prep_kernel,
        grid=(bsz, seq // tm),
        in_specs=[
            pl.BlockSpec((None, tm, C_PAD), row),
            full2(gq), full2(gkv), full2(wn), full2(wr), full2(wqi),
            pl.BlockSpec(wuk.shape, lambda b, i: (0, 0, 0)),
            tab, tab, tab, tab,
        ],
        out_specs=[
            pl.BlockSpec((None, bq, C_HEADS, DSA_Q, C_KV_W), lambda b, i: (b, i, 0, 0, 0)),
            pl.BlockSpec((None, tm, C_KV_W), row),
            pl.BlockSpec((None, tm, IDX_HEADS * IDX_DIM), row),
            pl.BlockSpec((None, tm, LANES), row),
            pl.BlockSpec((None, tm, LANES), row),
        ],
        out_shape=[
            jax.ShapeDtypeStruct((bsz, nq, C_HEADS, DSA_Q, C_KV_W), BF16),
            jax.ShapeDtypeStruct((bsz, seq, C_KV_W), BF16),
            jax.ShapeDtypeStruct((bsz, seq, IDX_HEADS * IDX_DIM), BF16),
            jax.ShapeDtypeStruct((bsz, seq, LANES), BF16),
            jax.ShapeDtypeStruct((bsz, seq, LANES), F32),
        ],
        compiler_params=_cparams(("parallel", "parallel")),
        name="c_prep",
    )(pc, gq, gkv, wn, wr, wqi, wuk, cc, sc, cp, sp)


def _sort_key(x):
    b = pltpu.bitcast(x, I32)
    return b ^ (lax.shift_right_arithmetic(b, 31) & 0x7FFFFFFF)


def _dsa_kernel(qf_ref, kv_ref, qi_ref, ki_ref, wi_ref, wuv_ref, o_ref,
                key_ref, bias_ref, pfx_ref, m_ref, l_ref, acc_ref, *, k_sel, idx_bits):
    i = pl.program_id(1)
    tq, kc = DSA_Q, DSA_KC
    nch = (i * tq) // kc + 1
    row_pos = i * tq + lax.broadcasted_iota(I32, (tq, kc), 0)
    col_in = lax.broadcasted_iota(I32, (tq, kc), 1)

    lane = lax.broadcasted_iota(I32, (tq, LANES), 1)
    parts, wcols = [], []
    for h in range(IDX_HEADS):
        tile = qi_ref[:, (h // 2) * LANES:(h // 2 + 1) * LANES]
        keep = (lane < IDX_DIM) if h % 2 == 0 else (lane >= IDX_DIM)
        parts.append(jnp.where(keep, tile, jnp.zeros_like(tile)))
        wcols.append(wi_ref[:, h:h + 1])
    qi = jnp.concatenate(parts, axis=0)
    wcol = jnp.concatenate(wcols, axis=0)

    def score_body(c, carry):
        start = pl.multiple_of(c * kc, kc)
        kic = ki_ref[pl.ds(start, kc), :]
        d = lax.dot_general(qi, kic, (((1,), (1,)), ((), ())), preferred_element_type=F32)
        d = jnp.maximum(d, 0.0) * wcol
        sc = jnp.sum(d.reshape(IDX_HEADS, tq, kc), axis=0) * (IDX_DIM ** -0.5)
        causal = (c * kc + col_in) <= row_pos
        key_ref[c] = jnp.where(causal, _sort_key(sc), INT_MIN)
        return carry

    lax.fori_loop(0, nch, score_body, 0)

    def count(pred):
        def body(c, cnt):
            hit = jnp.where(pred(key_ref[c], c), 1, 0)
            for g in range(kc // LANES):
                cnt = cnt + hit[:, g * LANES:(g + 1) * LANES]
            return cnt
        cnt = lax.fori_loop(0, nch, body, jnp.zeros((tq, LANES), I32))
        return jnp.sum(cnt, axis=-1, keepdims=True)

    def bit_body(it, thr):
        cand = thr ^ lax.shift_left(jnp.int32(1), 31 - it)
        n_ge = count(lambda key, c: key >= cand)
        return jnp.where(n_ge >= k_sel, cand, thr)

    thr = lax.fori_loop(0, 32, bit_body, jnp.full((tq, 1), INT_MIN, I32))
    thr = jnp.maximum(thr, INT_MIN + 1)
    n_gt = count(lambda key, c: key > thr)
    n_ge = count(lambda key, c: key >= thr)
    need = k_sel - n_gt
    pfx_ref[...] = jnp.full_like(pfx_ref, 2 ** 30)

    @pl.when(jnp.max(n_ge) > k_sel)
    def _():
        def pbit(it, pfx):
            cand = pfx + lax.shift_left(jnp.int32(1), idx_bits - 1 - it)
            n = count(lambda key, c: jnp.logical_and(key == thr, (c * kc + col_in) < cand))
            return jnp.where(n < need, cand, pfx)
        pfx_ref[...] = lax.fori_loop(0, idx_bits, pbit, jnp.zeros((tq, 1), I32))

    pfx = pfx_ref[...]

    def bias_body(c, carry):
        key = key_ref[c]
        tie = jnp.where((c * kc + col_in) <= pfx, 0.0, NEG)
        bias_ref[c] = jnp.where(key > thr, 0.0, jnp.where(key == thr, tie, NEG))
        return carry

    lax.fori_loop(0, nch, bias_body, 0)

    q = qf_ref[...].reshape(C_HEADS * tq, C_KV_W)
    m_ref[...] = jnp.full_like(m_ref, NEG)
    l_ref[...] = jnp.zeros_like(l_ref)
    acc_ref[...] = jnp.zeros_like(acc_ref)

    def att_body(c, carry):
        start = pl.multiple_of(c * kc, kc)
        kvc = kv_ref[pl.ds(start, kc), :]
        s = lax.dot_general(q, kvc, (((1,), (1,)), ((), ())), preferred_element_type=F32)
        s = (s.reshape(C_HEADS, tq, kc) + bias_ref[c][None]).reshape(C_HEADS * tq, kc)
        m_old = m_ref[...]
        m_new = jnp.maximum(m_old, jnp.max(s, axis=-1, keepdims=True))
        alpha = jnp.exp(m_old - m_new)
        p = jnp.exp(s - m_new)
        l_ref[...] = alpha * l_ref[...] + jnp.sum(p, axis=-1, keepdims=True)
        acc_ref[...] = alpha * acc_ref[...] + jnp.dot(p.astype(BF16), kvc[:, 0:C_KV_RANK],
                                                      preferred_element_type=F32)
        m_ref[...] = m_new
        return carry

    lax.fori_loop(0, nch, att_body, 0)

    o_lat = (acc_ref[...] / l_ref[...]).astype(BF16).reshape(C_HEADS, tq, C_KV_RANK)
    for j in range(C_HEADS // 2):
        o_ref[:, j * LANES:(j + 1) * LANES] = (
            jnp.dot(o_lat[2 * j], wuv_ref[2 * j], preferred_element_type=F32)
            + jnp.dot(o_lat[2 * j + 1], wuv_ref[2 * j + 1], preferred_element_type=F32)
        ).astype(o_ref.dtype)


def _dsa(qf, kv, qi, ki, wi, wuv):
    bsz, nq = qf.shape[0], qf.shape[1]
    seq = kv.shape[1]
    k_sel = min(TOPK_MAX, seq // 4)
    nck = -(-seq // DSA_KC)
    idx_bits = max(1, (seq - 1).bit_length())
    kern = functools.partial(_dsa_kernel, k_sel=k_sel, idx_bits=idx_bits)
    blk = lambda b, i: (b, i, 0)
    per_b = lambda b, i: (b, 0, 0)
    return pl.pallas_call(
        kern,
        grid=(bsz, nq),
        in_specs=[
            pl.BlockSpec((None, None, C_HEADS, DSA_Q, C_KV_W), lambda b, i: (b, i, 0, 0, 0)),
            pl.BlockSpec((None, seq, C_KV_W), per_b),
            pl.BlockSpec((None, DSA_Q, IDX_HEADS * IDX_DIM), blk),
            pl.BlockSpec((None, seq, LANES), per_b),
            pl.BlockSpec((None, DSA_Q, LANES), blk),
            pl.BlockSpec(wuv.shape, lambda b, i: (0, 0, 0)),
        ],
        out_specs=pl.BlockSpec((None, DSA_Q, C_HEADS * C_VDIM), blk),
        out_shape=jax.ShapeDtypeStruct((bsz, seq, C_HEADS * C_VDIM), BF16),
        scratch_shapes=[
            pltpu.VMEM((nck, DSA_Q, DSA_KC), I32),
            pltpu.VMEM((nck, DSA_Q, DSA_KC), F32),
            pltpu.VMEM((DSA_Q, 1), I32),
            pltpu.VMEM((C_HEADS * DSA_Q, 1), F32),
            pltpu.VMEM((C_HEADS * DSA_Q, 1), F32),
            pltpu.VMEM((C_HEADS * DSA_Q, C_KV_RANK), F32),
        ],
        compiler_params=_cparams(("parallel", "arbitrary")),
        name="dsa",
    )(qf, kv, qi, ki, wi, wuv)


def _prep_c_weights(w_in_c, w_uq, w_uk, w_uv):
    d = w_in_c.shape[0]
    cq_w, ckv_w, kr_w, kidx_w, widx_w = jnp.split(
        w_in_c, [C_Q_RANK, C_Q_RANK + C_KV_RANK, C_Q_RANK + C_KV_RANK + C_ROPE,
                 C_Q_RANK + C_KV_RANK + C_ROPE + IDX_DIM], axis=1)
    zeros = lambda n: jnp.zeros((d, n), w_in_c.dtype)
    w_c = jnp.concatenate([cq_w, ckv_w, kr_w, zeros(LANES - C_ROPE), kidx_w, kidx_w,
                           widx_w, zeros(LANES - IDX_HEADS)], axis=1)
    uq = w_uq.reshape(C_Q_RANK, C_HEADS, C_NOPE + C_ROPE)
    w_nope = uq[:, :, :C_NOPE].reshape(C_Q_RANK, C_HEADS * C_NOPE)
    w_rope = jnp.pad(uq[:, :, C_NOPE:], ((0, 0), (0, 0), (0, LANES - C_ROPE)))
    w_rope = w_rope.reshape(C_Q_RANK, C_HEADS * LANES)
    ukt = jnp.transpose(w_uk, (1, 2, 0))
    uk_pad = jnp.stack([
        jnp.concatenate([ukt[h], jnp.zeros_like(ukt[h])] if h % 2 == 0
                        else [jnp.zeros_like(ukt[h]), ukt[h]], axis=0)
        for h in range(C_HEADS)])
    uvh = jnp.transpose(w_uv, (1, 0, 2))
    uv_pad = jnp.stack([
        jnp.concatenate([uvh[h], jnp.zeros_like(uvh[h])] if h % 2 == 0
                        else [jnp.zeros_like(uvh[h]), uvh[h]], axis=1)
        for h in range(C_HEADS)])
    return (w_c.astype(BF16), w_nope.astype(BF16), w_rope.astype(BF16),
            uk_pad.astype(BF16), uv_pad.astype(BF16))


def _mixer_ab(h, g_pre, w_in, w_out, g_post, lam_params, g_sub, lam_init, tabs):
    cos_a, sin_a = tabs
    rope_groups = list(range(0, 8)) + list(range(12, 20))
    qkv = _rms_proj(h, g_pre, w_in.astype(BF16), cos_a, sin_a, rope_groups, HEAD_DIM // 2, BF16)
    ao = _pair_attn(qkv, 0, 4, 8, _causal_bias(ATT_T), lam_params, g_sub.reshape(1, A_VDIM),
                    windowed=False, diff=True, lam_init=lam_init)
    bo = _pair_attn(qkv, 12, 16, 20, _dilated_bias(ATT_T), lam_params, g_sub.reshape(1, A_VDIM),
                    windowed=True, diff=False, lam_init=lam_init)
    return _outproj_res([ao, bo], w_out.astype(BF16), g_post, h)


def _mixer_c(h, g_pre, w_in_c, g_cq, g_ckv, w_uq, w_qi, w_uk, w_uv, w_out, g_post, tabs):
    cos_c, sin_c, cos_p, sin_p = tabs
    w_c, w_nope, w_rope, uk_pad, uv_pad = _prep_c_weights(w_in_c, w_uq, w_uk, w_uv)
    pc = _rms_proj(h, g_pre, w_c, cos_c, sin_c, (), C_ROPE // 2, F32)
    qf, kv, qi, ki, wi = _c_prep(pc, g_cq.reshape(1, -1), g_ckv.reshape(1, -1), w_nope, w_rope,
                                 w_qi.astype(BF16), uk_pad, cos_c, sin_c, cos_p, sin_p)
    o = _dsa(qf, kv, qi, ki, wi, uv_pad)
    return _outproj_res([o], w_out.astype(BF16), g_post, h)


def kernel(x, p, g_mix_pre, g_mix_post, g_mlp_pre, g_mlp_post, w_mlp_in, w_mlp_out, w_ple_proj, w_ple_gate, w_in_ab, w_out_ab, diff_lq1, diff_lk1, diff_lq2, diff_lk2, g_diff_sub, w_in_c, g_cq, g_ckv, w_uq, w_qi, w_uk, w_uv, w_out_c):
    bsz, seq, d = x.shape
    depth = p.shape[0]
    tabs_a = _rope_tables(seq, HEAD_DIM // 2, HEAD_DIM, HEAD_DIM)
    tabs_c = (_rope_tables(seq, C_ROPE // 2, C_ROPE, C_ROPE)
              + _rope_tables(seq, IDX_ROPE // 2, IDX_DIM, IDX_ROPE))
    h = x
    for i in range(depth):
        j = i // 2
        if i % 2 == 0:
            lam_init = 0.8 - 0.6 * math.exp(-0.3 * i)
            lam_params = jnp.stack([diff_lq1[j], diff_lk1[j], diff_lq2[j], diff_lk2[j]])
            h = _mixer_ab(h, g_mix_pre[i], w_in_ab[j], w_out_ab[j], g_mix_post[i], lam_params,
                          g_diff_sub[j], lam_init, tabs_a)
        else:
            h = _mixer_c(h, g_mix_pre[i], w_in_c[j], g_cq[j], g_ckv[j], w_uq[j], w_qi[j],
                         w_uk[j], w_uv[j], w_out_c[j], g_mix_post[i], tabs_c)
        hf = _mlp(h.reshape(bsz * seq, d), g_mlp_pre[i], w_mlp_in[i].astype(BF16),
                  w_mlp_out[i].astype(BF16), g_mlp_post[i])
        hf = _ple(hf, p[i].reshape(bsz * seq, -1), w_ple_gate[i].astype(BF16),
                  w_ple_proj[i].astype(BF16))
        h = hf.reshape(bsz, seq, d)
    return h
```

```python
import functools
import math

import jax
import jax.numpy as jnp
from jax import lax
from jax.experimental import pallas as pl
from jax.experimental.pallas import tpu as pltpu

F32 = jnp.float32
BF16 = jnp.bfloat16
I32 = jnp.int32

LANES = 128
SUBLANES = 8
VMEM_LIMIT = 56 * 1024 * 1024

D_MODEL = 1024
HEAD_DIM = 64
A_HEADS = 4
A_VDIM = 2 * HEAD_DIM
B_HEADS = 8
B_BRANCHES = ((128, 1), (512, 4), (2048, 16))
C_HEADS = 16
C_Q_RANK = 384
C_KV_RANK = 256
C_NOPE = 64
C_ROPE = 32
C_VDIM = 64
C_SCALE = (C_NOPE + C_ROPE) ** -0.5
IDX_HEADS = 8
IDX_DIM = 64
IDX_ROPE = 32
TOPK_MAX = 256
D_FF = 4 * D_MODEL
PLE_DIM = 256
ROPE_THETA = 10000.0
EPS = 1e-6
NEG = -1e30
INT_MIN = -2 ** 31

C_PAD = 1024
DSA_T = 512
ATT_T = 512
PAIRS = 4


def _cparams(sem):
    return pltpu.CompilerParams(dimension_semantics=sem, vmem_limit_bytes=VMEM_LIMIT)


def _rms_f32(x, g):
    return x * lax.rsqrt(jnp.mean(x * x, axis=-1, keepdims=True) + EPS) * g


def _dot_nt(a, b):
    return lax.dot_general(a, b, (((1,), (1,)), ((), ())), preferred_element_type=F32)


def _rope_tile(y, cos, sin, half):
    lane = lax.broadcasted_iota(I32, y.shape, 1)
    first = (lane % (2 * half)) < half
    partner = jnp.where(first, pltpu.roll(y, LANES - half, 1), pltpu.roll(y, half, 1))
    return y * cos + partner * sin


def _rope_tables(seq, half, period, start, rot):
    inv = ROPE_THETA ** (-jnp.arange(half, dtype=F32) / half)
    ang = jnp.arange(seq).astype(F32)[:, None] * inv[None, :]
    g = jnp.arange(LANES) % period - start
    idx = g % half
    on = ((g >= 0) & (g < rot))[None, :]
    cos = jnp.where(on, jnp.cos(ang)[:, idx], 1.0)
    sgn = jnp.where(g % (2 * half) < half, -1.0, 1.0)[None, :]
    sin = jnp.where(on, jnp.sin(ang)[:, idx] * sgn, 0.0)
    return cos.astype(F32), sin.astype(F32)


def _proj_ab_kernel(x_ref, g_ref, w_ref, wvt_ref, cos_ref, sin_ref, o_ref, vt_ref, *, half):
    xn = _rms_f32(x_ref[...], g_ref[...]).astype(BF16)
    tm = xn.shape[0]
    n = w_ref.shape[1]
    slab = 512
    for s0 in range(0, n, slab):
        y = jnp.dot(xn, w_ref[:, s0:s0 + slab], preferred_element_type=F32)
        for c0 in range(0, slab, LANES):
            yc = _rope_tile(y[:, c0:c0 + LANES], cos_ref[...], sin_ref[...], half)
            o_ref[:, s0 + c0:s0 + c0 + LANES] = yc.astype(o_ref.dtype)
    tk = vt_ref.shape[-1]
    for hb in range(vt_ref.shape[0]):
        vt = _dot_nt(wvt_ref[hb * LANES:(hb + 1) * LANES, :], xn).astype(vt_ref.dtype)
        for s in range(tm // tk):
            vt_ref[hb, s] = vt[:, s * tk:(s + 1) * tk]


def _proj_ab(x, g, w_qk, w_vt, cos, sin, half, tk, tm=512):
    bsz, seq, d = x.shape
    n = w_qk.shape[1]
    nvb = w_vt.shape[0] // LANES
    return pl.pallas_call(
        functools.partial(_proj_ab_kernel, half=half),
        grid=(bsz, seq // tm),
        in_specs=[
            pl.BlockSpec((None, tm, d), lambda b, i: (b, i, 0)),
            pl.BlockSpec((1, d), lambda b, i: (0, 0)),
            pl.BlockSpec((d, n), lambda b, i: (0, 0)),
            pl.BlockSpec(w_vt.shape, lambda b, i: (0, 0)),
            pl.BlockSpec((tm, LANES), lambda b, i: (i, 0)),
            pl.BlockSpec((tm, LANES), lambda b, i: (i, 0)),
        ],
        out_specs=[
            pl.BlockSpec((None, tm, n), lambda b, i: (b, i, 0)),
            pl.BlockSpec((None, nvb, tm // tk, LANES, tk), lambda b, i: (b, 0, i, 0, 0)),
        ],
        out_shape=[
            jax.ShapeDtypeStruct((bsz, seq, n), BF16),
            jax.ShapeDtypeStruct((bsz, nvb, seq // tk, LANES, tk), BF16),
        ],
        compiler_params=_cparams(("parallel", "parallel")),
        name="proj_ab",
    )(x, g.reshape(1, d), w_qk, w_vt, cos, sin)


def _proj_c_kernel(x_ref, g_ref, w_ref, o_ref):
    xn = _rms_f32(x_ref[...], g_ref[...]).astype(BF16)
    o_ref[...] = jnp.dot(xn, w_ref[...], preferred_element_type=F32)


def _proj_c(x, g, w, tm=512):
    bsz, seq, d = x.shape
    n = w.shape[1]
    return pl.pallas_call(
        _proj_c_kernel,
        grid=(bsz, seq // tm),
        in_specs=[
            pl.BlockSpec((None, tm, d), lambda b, i: (b, i, 0)),
            pl.BlockSpec((1, d), lambda b, i: (0, 0)),
            pl.BlockSpec((d, n), lambda b, i: (0, 0)),
        ],
        out_specs=pl.BlockSpec((None, tm, n), lambda b, i: (b, i, 0)),
        out_shape=jax.ShapeDtypeStruct((bsz, seq, n), F32),
        compiler_params=_cparams(("parallel", "parallel")),
        name="proj_c",
    )(x, g.reshape(1, d), w)


def _outproj_kernel(*refs, n_in):
    y_refs = refs[:n_in]
    w_ref, g_ref, h_ref, o_ref = refs[n_in:]
    acc = None
    off = 0
    for y_ref in y_refs:
        k = y_ref.shape[1]
        part = jnp.dot(y_ref[...], w_ref[off:off + k, :], preferred_element_type=F32)
        acc = part if acc is None else acc + part
        off += k
    o_ref[...] = h_ref[...] + _rms_f32(acc, g_ref[...])


def _outproj_res(ys, w, g, h, tm=512):
    bsz, seq, d = h.shape
    in_specs = [pl.BlockSpec((None, tm, y.shape[2]), lambda b, i: (b, i, 0)) for y in ys]
    in_specs += [
        pl.BlockSpec(w.shape, lambda b, i: (0, 0)),
        pl.BlockSpec((1, d), lambda b, i: (0, 0)),
        pl.BlockSpec((None, tm, d), lambda b, i: (b, i, 0)),
    ]
    return pl.pallas_call(
        functools.partial(_outproj_kernel, n_in=len(ys)),
        grid=(bsz, seq // tm),
        in_specs=in_specs,
        out_specs=pl.BlockSpec((None, tm, d), lambda b, i: (b, i, 0)),
        out_shape=jax.ShapeDtypeStruct(h.shape, F32),
        compiler_params=_cparams(("parallel", "parallel")),
        name="outproj_res",
    )(*ys, w, g.reshape(1, d), h)


def _mlp_kernel(h_ref, g1_ref, w1_ref, w2_ref, g2_ref, o_ref, xn_ref, acc_ref):
    j = pl.program_id(1)

    @pl.when(j == 0)
    def _():
        xn_ref[...] = _rms_f32(h_ref[...], g1_ref[...]).astype(BF16)
        acc_ref[...] = jnp.zeros_like(acc_ref)

    a = jnp.dot(xn_ref[...], w1_ref[...], preferred_element_type=F32)
    a = jnp.square(jnp.maximum(a, 0.0)).astype(BF16)
    acc_ref[...] += jnp.dot(a, w2_ref[...], preferred_element_type=F32)

    @pl.when(j == pl.num_programs(1) - 1)
    def _():
        o_ref[...] = h_ref[...] + _rms_f32(acc_ref[...], g2_ref[...])


def _mlp(h, g1, w1, w2, g2, tm=1024, tf=512):
    t, d = h.shape
    ff = w1.shape[1]
    return pl.pallas_call(
        _mlp_kernel,
        grid=(t // tm, ff // tf),
        in_specs=[
            pl.BlockSpec((tm, d), lambda i, j: (i, 0)),
            pl.BlockSpec((1, d), lambda i, j: (0, 0)),
            pl.BlockSpec((d, tf), lambda i, j: (0, j)),
            pl.BlockSpec((tf, d), lambda i, j: (j, 0)),
            pl.BlockSpec((1, d), lambda i, j: (0, 0)),
        ],
        out_specs=pl.BlockSpec((tm, d), lambda i, j: (i, 0)),
        out_shape=jax.ShapeDtypeStruct((t, d), F32),
        scratch_shapes=[pltpu.VMEM((tm, d), BF16), pltpu.VMEM((tm, d), F32)],
        compiler_params=_cparams(("parallel", "arbitrary")),
        name="mlp",
    )(h, g1.reshape(1, d), w1, w2, g2.reshape(1, d))


def _ple_kernel(h_ref, p_ref, wg_ref, wp_ref, o_ref):
    h = h_ref[...]
    gate = jax.nn.sigmoid(jnp.dot(h.astype(BF16), wg_ref[...], preferred_element_type=F32))
    emb = jnp.dot(p_ref[...].astype(BF16), wp_ref[...], preferred_element_type=F32)
    o_ref[...] = h + gate * emb


def _ple(h, p, wg, wp, tm=512):
    t, d = h.shape
    pd = p.shape[1]
    return pl.pallas_call(
        _ple_kernel,
        grid=(t // tm,),
        in_specs=[
            pl.BlockSpec((tm, d), lambda i: (i, 0)),
            pl.BlockSpec((tm, pd), lambda i: (i, 0)),
            pl.BlockSpec((d, d), lambda i: (0, 0)),
            pl.BlockSpec((pd, d), lambda i: (0, 0)),
        ],
        out_specs=pl.BlockSpec((tm, d), lambda i: (i, 0)),
        out_shape=jax.ShapeDtypeStruct((t, d), F32),
        compiler_params=_cparams(("parallel",)),
        name="ple",
    )(h, p, wg, wp)


def _flash_step(s_t, vt, m_ref, l_ref, acc_ref):
    m_old = m_ref[...]
    m_new = jnp.maximum(m_old, jnp.max(s_t, axis=0, keepdims=True))
    alpha = jnp.exp(m_old - m_new)
    p = jnp.exp(s_t - m_new)
    l_ref[...] = alpha * l_ref[...] + jnp.sum(p, axis=0, keepdims=True)
    acc_ref[...] = alpha * acc_ref[...] + jnp.dot(vt, p.astype(BF16), preferred_element_type=F32)
    m_ref[...] = m_new


def _flash_init(m_ref, l_ref, acc_ref):
    m_ref[...] = jnp.full_like(m_ref, NEG)
    l_ref[...] = jnp.zeros_like(l_ref)
    acc_ref[...] = jnp.zeros_like(acc_ref)


def _pair_attn_kernel(q_ref, k_ref, vt_ref, bias_ref, lam_ref, gsub_ref, o_ref,
                      m_ref, l_ref, acc_ref, *, windowed, diff, lam_init):
    i = pl.program_id(2)
    t = q_ref.shape[0]
    nb = bias_ref.shape[0]
    lane = lax.broadcasted_iota(I32, (t, LANES), 1)
    q = q_ref[...] * jnp.asarray(HEAD_DIM ** -0.5, BF16)
    zero = jnp.zeros_like(q)
    qs = jnp.concatenate([jnp.where(lane < HEAD_DIM, q, zero),
                          jnp.where(lane >= HEAD_DIM, q, zero)], axis=0)
    _flash_init(m_ref, l_ref, acc_ref)

    def body(j, carry):
        start = pl.multiple_of(j * t, t)
        bias = bias_ref[jnp.minimum(i - j, nb - 1)]
        s_t = _dot_nt(k_ref[pl.ds(start, t), :], qs) + jnp.concatenate([bias, bias], axis=1)
        _flash_step(s_t, vt_ref[j], m_ref, l_ref, acc_ref)
        return carry

    lo = jnp.maximum(i - (nb - 1), 0) if windowed else 0
    lax.fori_loop(lo, i + 1, body, 0)

    o_t = acc_ref[...] / l_ref[...]
    oa, ob = o_t[:, :t], o_t[:, t:]
    if diff:
        lp = lam_ref[...]
        lam = (jnp.exp(jnp.sum(lp[0:1] * lp[1:2], axis=-1, keepdims=True))
               - jnp.exp(jnp.sum(lp[2:3] * lp[3:4], axis=-1, keepdims=True)) + lam_init)
        d = oa - lam * ob
        y = d * lax.rsqrt(jnp.mean(d * d, axis=0, keepdims=True) + EPS) * gsub_ref[...]
        y = y * (1.0 - lam_init)
    else:
        row = lax.broadcasted_iota(I32, (LANES, t), 0)
        y = jnp.where(row < HEAD_DIM, oa, ob)
    o_ref[...] = y.T.astype(o_ref.dtype)


def _pair_attn(qk, vt, qcol, kcol, vblk, bias, lam_params, gsub, *, windowed, diff, lam_init):
    bsz, seq, _ = qk.shape
    t = ATT_T
    kern = functools.partial(_pair_attn_kernel, windowed=windowed, diff=diff, lam_init=lam_init)
    return pl.pallas_call(
        kern,
        grid=(bsz, PAIRS, seq // t),
        in_specs=[
            pl.BlockSpec((None, t, LANES), lambda b, h, i: (b, i, qcol + h)),
            pl.BlockSpec((None, seq, LANES), lambda b, h, i: (b, 0, kcol + h)),
            pl.BlockSpec((None, None, seq // t, LANES, t), lambda b, h, i: (b, vblk + h, 0, 0, 0)),
            pl.BlockSpec(bias.shape, lambda b, h, i: (0, 0, 0)),
            pl.BlockSpec(lam_params.shape, lambda b, h, i: (0, 0)),
            pl.BlockSpec(gsub.shape, lambda b, h, i: (0, 0)),
        ],
        out_specs=pl.BlockSpec((None, t, LANES), lambda b, h, i: (b, i, h)),
        out_shape=jax.ShapeDtypeStruct((bsz, seq, PAIRS * LANES), BF16),
        scratch_shapes=[pltpu.VMEM((1, 2 * t), F32), pltpu.VMEM((1, 2 * t), F32),
                        pltpu.VMEM((LANES, 2 * t), F32)],
        compiler_params=_cparams(("parallel", "parallel", "arbitrary")),
        name="diff_attn" if diff else "dilated_attn",
    )(qk, qk, vt, bias, lam_params, gsub)


def _causal_bias(t):
    r = jnp.arange(t)
    tri = jnp.where(r[None, :] >= r[:, None], 0.0, NEG).astype(F32)
    return jnp.stack([tri, jnp.zeros((t, t), F32)])


def _dilated_bias(t):
    max_dist = max(w for w, _ in B_BRANCHES)
    n_off = max_dist // t + 1
    r = jnp.arange(t)
    tiles = []
    for o in range(n_off):
        dist = o * t + r[None, :] - r[:, None]
        cnt = jnp.zeros((t, t), F32)
        for w, d in B_BRANCHES:
            cnt = cnt + ((dist >= 0) & (dist <= w) & (dist % d == 0)).astype(F32)
        tiles.append(jnp.where(cnt > 0, jnp.log(jnp.maximum(cnt, 1.0)), NEG))
    return jnp.stack(tiles).astype(F32)


def _c_prep_kernel(pc_ref, gq_ref, gkv_ref, wq_ref, wk_ref, wvt_ref, wqi_ref,
                   ch_ref, sh_ref, cp_ref, sp_ref,
                   q_ref, k_ref, vt_ref, qi_ref, ki_ref, wit_ref):
    half = C_ROPE // 2
    cqn = _rms_f32(pc_ref[:, 0:C_Q_RANK], gq_ref[...]).astype(BF16)
    c0 = C_Q_RANK
    ckvn = _rms_f32(pc_ref[:, c0:c0 + C_KV_RANK], gkv_ref[...]).astype(BF16)
    c0 += C_KV_RANK
    kr = _rope_tile(pc_ref[:, c0:c0 + LANES], ch_ref[...], sh_ref[...], half)
    c0 += LANES
    ki_ref[...] = _rope_tile(pc_ref[:, c0:c0 + LANES], cp_ref[...], sp_ref[...], half).astype(BF16)
    c0 += LANES
    wit_ref[...] = (pc_ref[:, c0:c0 + LANES] * (IDX_HEADS ** -0.5)).T

    qi = jnp.dot(cqn, wqi_ref[...], preferred_element_type=F32)
    for c in range(0, IDX_HEADS * IDX_DIM, LANES):
        qi_ref[:, c:c + LANES] = _rope_tile(qi[:, c:c + LANES], cp_ref[...], sp_ref[...],
                                            half).astype(BF16)

    for h in range(C_HEADS):
        cols = slice(h * LANES, (h + 1) * LANES)
        qh = jnp.dot(cqn, wq_ref[:, cols], preferred_element_type=F32)
        q_ref[:, cols] = (_rope_tile(qh, ch_ref[...], sh_ref[...], half) * C_SCALE).astype(BF16)
        kh = jnp.dot(ckvn, wk_ref[:, cols], preferred_element_type=F32)
        k_ref[:, cols] = (kh + kr).astype(BF16)
    for hp in range(C_HEADS // 2):
        vt_ref[hp] = _dot_nt(wvt_ref[hp * LANES:(hp + 1) * LANES, :], ckvn).astype(BF16)


def _c_prep(pc, gq, gkv, wq, wk, wvt, wqi, ch, sh, cp, sp):
    bsz, seq, _ = pc.shape
    tm = DSA_T
    hw = C_HEADS * LANES
    row = lambda b, i: (b, i, 0)
    tab = pl.BlockSpec((tm, LANES), lambda b, i: (i, 0))
    full2 = lambda a: pl.BlockSpec(a.shape, lambda b, i: (0, 0))
    return pl.pallas_call(
        _c_prep_kernel,
        grid=(bsz, seq // tm),
        in_specs=[
            pl.BlockSpec((None, tm, C_PAD), row),
            full2(gq), full2(gkv), full2(wq), full2(wk), full2(wvt), full2(wqi),
            tab, tab, tab, tab,
        ],
        out_specs=[
            pl.BlockSpec((None, tm, hw), row),
            pl.BlockSpec((None, tm, hw), row),
            pl.BlockSpec((None, C_HEADS // 2, None, LANES, tm), lambda b, i: (b, 0, i, 0, 0)),
            pl.BlockSpec((None, tm, IDX_HEADS * IDX_DIM), row),
            pl.BlockSpec((None, tm, LANES), row),
            pl.BlockSpec((None, LANES, tm), lambda b, i: (b, 0, i)),
        ],
        out_shape=[
            jax.ShapeDtypeStruct((bsz, seq, hw), BF16),
            jax.ShapeDtypeStruct((bsz, seq, hw), BF16),
            jax.ShapeDtypeStruct((bsz, C_HEADS // 2, seq // tm, LANES, tm), BF16),
            jax.ShapeDtypeStruct((bsz, seq, IDX_HEADS * IDX_DIM), BF16),
            jax.ShapeDtypeStruct((bsz, seq, LANES), BF16),
            jax.ShapeDtypeStruct((bsz, LANES, seq), F32),
        ],
        compiler_params=_cparams(("parallel", "parallel")),
        name="c_prep",
    )(pc, gq, gkv, wq, wk, wvt, wqi, ch, sh, cp, sp)


def _sort_key(x):
    b = pltpu.bitcast(x, I32)
    return b ^ (lax.shift_right_arithmetic(b, 31) & 0x7FFFFFFF)


def _dsa_select(iq, qi_ref, ki_ref, wit_ref, qm_ref, key_ref, bias_ref, pfx_ref, *,
                k_sel, idx_bits):
    t = DSA_T
    nch = iq + 1
    q_pos = iq * t + lax.broadcasted_iota(I32, (t, t), 1)
    k_in = lax.broadcasted_iota(I32, (t, t), 0)

    lane = lax.broadcasted_iota(I32, (t, LANES), 1)
    for h in range(IDX_HEADS):
        tile = qi_ref[:, (h // 2) * LANES:(h // 2 + 1) * LANES]
        keep = (lane < IDX_DIM) if h % 2 == 0 else (lane >= IDX_DIM)
        qm_ref[h] = jnp.where(keep, tile, jnp.zeros_like(tile))

    def score_body(c, carry):
        start = pl.multiple_of(c * t, t)
        kic = ki_ref[pl.ds(start, t), :]
        sc = jnp.zeros((t, t), F32)
        for h in range(IDX_HEADS):
            sc = sc + jnp.maximum(_dot_nt(kic, qm_ref[h]), 0.0) * wit_ref[h:h + 1, :]
        sc = sc * (IDX_DIM ** -0.5)
        causal = (c * t + k_in) <= q_pos
        key_ref[c] = jnp.where(causal, _sort_key(sc), INT_MIN)
        return carry

    lax.fori_loop(0, nch, score_body, 0)

    def count(pred):
        def body(c, cnt):
            hit = jnp.where(pred(key_ref[c], c), 1, 0)
            return cnt + jnp.sum(hit.reshape(t // SUBLANES, SUBLANES, t), axis=0)
        cnt = lax.fori_loop(0, nch, body, jnp.zeros((SUBLANES, t), I32))
        return jnp.sum(cnt, axis=0, keepdims=True)

    def bit_body(it, thr):
        cand = thr ^ lax.shift_left(jnp.int32(1), 31 - it)
        n_ge = count(lambda key, c: key >= cand)
        return jnp.where(n_ge >= k_sel, cand, thr)

    thr = lax.fori_loop(0, 32, bit_body, jnp.full((1, t), INT_MIN, I32))
    thr = jnp.maximum(thr, INT_MIN + 1)
    n_gt = count(lambda key, c: key > thr)
    n_ge = count(lambda key, c: key >= thr)
    need = k_sel - n_gt
    pfx_ref[...] = jnp.full_like(pfx_ref, 2 ** 30)

    @pl.when(jnp.max(n_ge) > k_sel)
    def _():
        def pbit(it, pfx):
            cand = pfx + lax.shift_left(jnp.int32(1), idx_bits - 1 - it)
            n = count(lambda key, c: jnp.logical_and(key == thr, (c * t + k_in) < cand))
            return jnp.where(n < need, cand, pfx)
        pfx_ref[...] = lax.fori_loop(0, idx_bits, pbit, jnp.zeros((1, t), I32))

    pfx = pfx_ref[...]

    def bias_body(c, carry):
        key = key_ref[c]
        tie = jnp.where((c * t + k_in) <= pfx, 0.0, NEG)
        bias_ref[c] = jnp.where(key > thr, 0.0, jnp.where(key == thr, tie, NEG))
        return carry

    lax.fori_loop(0, nch, bias_body, 0)


def _dsa_kernel(q_ref, k_ref, vt_ref, qi_ref, ki_ref, wit_ref, o_ref,
                qm_ref, key_ref, bias_ref, pfx_ref, m_ref, l_ref, acc_ref, *, k_sel, idx_bits):
    iq = pl.program_id(1)
    t = DSA_T

    @pl.when(pl.program_id(2) == 0)
    def _():
        _dsa_select(iq, qi_ref, ki_ref, wit_ref, qm_ref, key_ref, bias_ref, pfx_ref,
                    k_sel=k_sel, idx_bits=idx_bits)

    q0 = q_ref[:, 0:LANES]
    q1 = q_ref[:, LANES:2 * LANES]
    _flash_init(m_ref, l_ref, acc_ref)

    def body(c, carry):
        start = pl.multiple_of(c * t, t)
        bias = bias_ref[c]
        s0 = _dot_nt(k_ref[pl.ds(start, t), 0:LANES], q0) + bias
        s1 = _dot_nt(k_ref[pl.ds(start, t), LANES:2 * LANES], q1) + bias
        _flash_step(jnp.concatenate([s0, s1], axis=1), vt_ref[c], m_ref, l_ref, acc_ref)
        return carry

    lax.fori_loop(0, iq + 1, body, 0)

    o_t = acc_ref[...] / l_ref[...]
    row = lax.broadcasted_iota(I32, (LANES, t), 0)
    o_ref[...] = jnp.where(row < C_VDIM, o_t[:, :t], o_t[:, t:]).T.astype(o_ref.dtype)


def _dsa(q, k, vt, qi, ki, wit):
    bsz, seq, _ = q.shape
    t = DSA_T
    nq = seq // t
    k_sel = min(TOPK_MAX, seq // 4)
    idx_bits = max(1, (seq - 1).bit_length())
    kern = functools.partial(_dsa_kernel, k_sel=k_sel, idx_bits=idx_bits)
    return pl.pallas_call(
        kern,
        grid=(bsz, nq, C_HEADS // 2),
        in_specs=[
            pl.BlockSpec((None, t, 2 * LANES), lambda b, i, h: (b, i, h)),
            pl.BlockSpec((None, seq, 2 * LANES), lambda b, i, h: (b, 0, h)),
            pl.BlockSpec((None, None, nq, LANES, t), lambda b, i, h: (b, h, 0, 0, 0)),
            pl.BlockSpec((None, t, IDX_HEADS * IDX_DIM), lambda b, i, h: (b, i, 0)),
            pl.BlockSpec((None, seq, LANES), lambda b, i, h: (b, 0, 0)),
            pl.BlockSpec((None, LANES, t), lambda b, i, h: (b, 0, i)),
        ],
        out_specs=pl.BlockSpec((None, t, LANES), lambda b, i, h: (b, i, h)),
        out_shape=jax.ShapeDtypeStruct((bsz, seq, C_HEADS * C_VDIM), BF16),
        scratch_shapes=[
            pltpu.VMEM((IDX_HEADS, t, LANES), BF16),
            pltpu.VMEM((nq, t, t), I32),
            pltpu.VMEM((nq, t, t), F32),
            pltpu.VMEM((1, t), I32),
            pltpu.VMEM((1, 2 * t), F32),
            pltpu.VMEM((1, 2 * t), F32),
            pltpu.VMEM((LANES, 2 * t), F32),
        ],
        compiler_params=_cparams(("parallel", "arbitrary", "arbitrary")),
        name="dsa",
    )(q, k, vt, qi, ki, wit)


def _prep_c_weights(w_in_c, w_uq, w_uk, w_uv):
    d = w_in_c.shape[0]
    cq_w, ckv_w, kr_w, kidx_w, widx_w = jnp.split(
        w_in_c, [C_Q_RANK, C_Q_RANK + C_KV_RANK, C_Q_RANK + C_KV_RANK + C_ROPE,
                 C_Q_RANK + C_KV_RANK + C_ROPE + IDX_DIM], axis=1)
    zeros = lambda n: jnp.zeros((d, n), w_in_c.dtype)
    w_c = jnp.concatenate([cq_w, ckv_w, zeros(C_NOPE), kr_w, zeros(LANES - C_NOPE - C_ROPE),
                           kidx_w, kidx_w, widx_w, zeros(LANES - IDX_HEADS)], axis=1)
    pad = LANES - C_NOPE - C_ROPE
    uq = w_uq.reshape(C_Q_RANK, C_HEADS, C_NOPE + C_ROPE)
    w_q = jnp.pad(uq, ((0, 0), (0, 0), (0, pad))).reshape(C_Q_RANK, C_HEADS * LANES)
    w_k = jnp.pad(w_uk, ((0, 0), (0, 0), (0, LANES - C_NOPE))).reshape(C_KV_RANK, C_HEADS * LANES)
    w_vt = w_uv.reshape(C_KV_RANK, C_HEADS * C_VDIM).T
    return w_c.astype(BF16), w_q.astype(BF16), w_k.astype(BF16), w_vt.astype(BF16)


def _mixer_ab(h, g_pre, w_in, w_out, g_post, lam_params, g_sub, lam_init, tabs):
    cos_a, sin_a = tabs
    aq, ak, av, bq, bk, bv = jnp.split(w_in, [512, 1024, 1536, 2048, 2560], axis=1)
    w_qk = jnp.concatenate([aq, ak, bq, bk], axis=1).astype(BF16)
    w_vt = jnp.concatenate([av, bv], axis=1).T.astype(BF16)
    qk, vt = _proj_ab(h, g_pre, w_qk, w_vt, cos_a, sin_a, HEAD_DIM // 2, ATT_T)
    gsub_col = g_sub.reshape(A_VDIM, 1)
    ao = _pair_attn(qk, vt, 0, 4, 0, _causal_bias(ATT_T), lam_params, gsub_col,
                    windowed=False, diff=True, lam_init=lam_init)
    bo = _pair_attn(qk, vt, 8, 12, 4, _dilated_bias(ATT_T), lam_params, gsub_col,
                    windowed=True, diff=False, lam_init=lam_init)
    return _outproj_res([ao, bo], w_out.astype(BF16), g_post, h)


def _mixer_c(h, g_pre, w_in_c, g_cq, g_ckv, w_uq, w_qi, w_uk, w_uv, w_out, g_post, tabs):
    cos_h, sin_h, cos_p, sin_p = tabs
    w_c, w_q, w_k, w_vt = _prep_c_weights(w_in_c, w_uq, w_uk, w_uv)
    pc = _proj_c(h, g_pre, w_c)
    q, k, vt, qi, ki, wit = _c_prep(pc, g_cq.reshape(1, -1), g_ckv.reshape(1, -1), w_q, w_k, w_vt,
                                    w_qi.astype(BF16), cos_h, sin_h, cos_p, sin_p)
    o = _dsa(q, k, vt, qi, ki, wit)
    return _outproj_res([o], w_out.astype(BF16), g_post, h)


def kernel(x, p, g_mix_pre, g_mix_post, g_mlp_pre, g_mlp_post, w_mlp_in, w_mlp_out, w_ple_proj, w_ple_gate, w_in_ab, w_out_ab, diff_lq1, diff_lk1, diff_lq2, diff_lk2, g_diff_sub, w_in_c, g_cq, g_ckv, w_uq, w_qi, w_uk, w_uv, w_out_c):
    bsz, seq, d = x.shape
    depth = p.shape[0]
    tabs_a = _rope_tables(seq, HEAD_DIM // 2, HEAD_DIM, 0, HEAD_DIM)
    tabs_c = (_rope_tables(seq, C_ROPE // 2, LANES, C_NOPE, C_ROPE)
              + _rope_tables(seq, IDX_ROPE // 2, IDX_DIM, 0, IDX_ROPE))
    h = x
    for i in range(depth):
        j = i // 2
        if i % 2 == 0:
            lam_init = 0.8 - 0.6 * math.exp(-0.3 * i)
            lam_params = jnp.stack([diff_lq1[j], diff_lk1[j], diff_lq2[j], diff_lk2[j]])
            h = _mixer_ab(h, g_mix_pre[i], w_in_ab[j], w_out_ab[j], g_mix_post[i], lam_params,
                          g_diff_sub[j], lam_init, tabs_a)
        else:
            h = _mixer_c(h, g_mix_pre[i], w_in_c[j], g_cq[j], g_ckv[j], w_uq[j], w_qi[j],
                         w_uk[j], w_uv[j], w_out_c[j], g_mix_post[i], tabs_c)
        hf = _mlp(h.reshape(bsz * seq, d), g_mlp_pre[i], w_mlp_in[i].astype(BF16),
                  w_mlp_out[i].astype(BF16), g_mlp_post[i])
        hf = _ple(hf, p[i].reshape(bsz * seq, -1), w_ple_gate[i].astype(BF16),
                  w_ple_proj[i].astype(BF16))
        h = hf.reshape(bsz, seq, d)
    return h
```

```python
import functools
import math

import jax
import jax.numpy as jnp
from jax import lax
from jax.experimental import pallas as pl
from jax.experimental.pallas import tpu as pltpu

F32 = jnp.float32
BF16 = jnp.bfloat16
I32 = jnp.int32

LANES = 128
SUBLANES = 8
VMEM_LIMIT = 56 * 1024 * 1024

D_MODEL = 1024
HEAD_DIM = 64
A_HEADS = 4
A_VDIM = 2 * HEAD_DIM
B_HEADS = 8
B_BRANCHES = ((128, 1), (512, 4), (2048, 16))
C_HEADS = 16
C_Q_RANK = 384
C_KV_RANK = 256
C_NOPE = 64
C_ROPE = 32
C_VDIM = 64
C_SCALE = (C_NOPE + C_ROPE) ** -0.5
IDX_HEADS = 8
IDX_DIM = 64
IDX_ROPE = 32
TOPK_MAX = 256
D_FF = 4 * D_MODEL
PLE_DIM = 256
ROPE_THETA = 10000.0
EPS = 1e-6
NEG = -1e30
LOG2E = 1.4426950408889634
SUM_ROWS = 16
INT_MIN = -2 ** 31

C_PAD = 1024
DSA_T = 512
ATT_T = 512
PAIRS = 4


def _cparams(sem):
    return pltpu.CompilerParams(dimension_semantics=sem, vmem_limit_bytes=VMEM_LIMIT)


def _rms_f32(x, g):
    return x * lax.rsqrt(jnp.mean(x * x, axis=-1, keepdims=True) + EPS) * g


def _dot_nt(a, b):
    return lax.dot_general(a, b, (((1,), (1,)), ((), ())), preferred_element_type=F32)


def _rope_tile(y, cos, sin, half):
    lane = lax.broadcasted_iota(I32, y.shape, 1)
    first = (lane % (2 * half)) < half
    partner = jnp.where(first, pltpu.roll(y, LANES - half, 1), pltpu.roll(y, half, 1))
    return y * cos + partner * sin


def _rope_tables(seq, half, period, start, rot):
    inv = ROPE_THETA ** (-jnp.arange(half, dtype=F32) / half)
    ang = jnp.arange(seq).astype(F32)[:, None] * inv[None, :]
    g = jnp.arange(LANES) % period - start
    idx = g % half
    on = ((g >= 0) & (g < rot))[None, :]
    cos = jnp.where(on, jnp.cos(ang)[:, idx], 1.0)
    sgn = jnp.where(g % (2 * half) < half, -1.0, 1.0)[None, :]
    sin = jnp.where(on, jnp.sin(ang)[:, idx] * sgn, 0.0)
    return cos.astype(F32), sin.astype(F32)


def _proj_ab_kernel(x_ref, g_ref, w_ref, wvt_ref, cos_ref, sin_ref, o_ref, vt_ref, *, half,
                    q_groups, q_scale):
    xn = _rms_f32(x_ref[...], g_ref[...]).astype(BF16)
    tm = xn.shape[0]
    n = w_ref.shape[1]
    slab = 512
    for s0 in range(0, n, slab):
        y = jnp.dot(xn, w_ref[:, s0:s0 + slab], preferred_element_type=F32)
        for c0 in range(0, slab, LANES):
            yc = _rope_tile(y[:, c0:c0 + LANES], cos_ref[...], sin_ref[...], half)
            if (s0 + c0) // LANES in q_groups:
                yc = yc * q_scale
            o_ref[:, s0 + c0:s0 + c0 + LANES] = yc.astype(o_ref.dtype)
    tk = vt_ref.shape[-1]
    for hb in range(vt_ref.shape[0]):
        vt = _dot_nt(wvt_ref[hb * LANES:(hb + 1) * LANES, :], xn).astype(vt_ref.dtype)
        for s in range(tm // tk):
            vt_ref[hb, s] = vt[:, s * tk:(s + 1) * tk]


def _proj_ab(x, g, w_qk, w_vt, cos, sin, half, tk, q_groups, q_scale, tm=512):
    bsz, seq, d = x.shape
    n = w_qk.shape[1]
    nvb = w_vt.shape[0] // LANES
    return pl.pallas_call(
        functools.partial(_proj_ab_kernel, half=half, q_groups=frozenset(q_groups),
                          q_scale=q_scale),
        grid=(bsz, seq // tm),
        in_specs=[
            pl.BlockSpec((None, tm, d), lambda b, i: (b, i, 0)),
            pl.BlockSpec((1, d), lambda b, i: (0, 0)),
            pl.BlockSpec((d, n), lambda b, i: (0, 0)),
            pl.BlockSpec(w_vt.shape, lambda b, i: (0, 0)),
            pl.BlockSpec((tm, LANES), lambda b, i: (i, 0)),
            pl.BlockSpec((tm, LANES), lambda b, i: (i, 0)),
        ],
        out_specs=[
            pl.BlockSpec((None, tm, n), lambda b, i: (b, i, 0)),
            pl.BlockSpec((None, nvb, tm // tk, LANES, tk), lambda b, i: (b, 0, i, 0, 0)),
        ],
        out_shape=[
            jax.ShapeDtypeStruct((bsz, seq, n), BF16),
            jax.ShapeDtypeStruct((bsz, nvb, seq // tk, LANES, tk), BF16),
        ],
        compiler_params=_cparams(("parallel", "parallel")),
        name="proj_ab",
    )(x, g.reshape(1, d), w_qk, w_vt, cos, sin)


def _proj_c_kernel(x_ref, g_ref, w_ref, o_ref):
    xn = _rms_f32(x_ref[...], g_ref[...]).astype(BF16)
    o_ref[...] = jnp.dot(xn, w_ref[...], preferred_element_type=F32)


def _proj_c(x, g, w, tm=512):
    bsz, seq, d = x.shape
    n = w.shape[1]
    return pl.pallas_call(
        _proj_c_kernel,
        grid=(bsz, seq // tm),
        in_specs=[
            pl.BlockSpec((None, tm, d), lambda b, i: (b, i, 0)),
            pl.BlockSpec((1, d), lambda b, i: (0, 0)),
            pl.BlockSpec((d, n), lambda b, i: (0, 0)),
        ],
        out_specs=pl.BlockSpec((None, tm, n), lambda b, i: (b, i, 0)),
        out_shape=jax.ShapeDtypeStruct((bsz, seq, n), F32),
        compiler_params=_cparams(("parallel", "parallel")),
        name="proj_c",
    )(x, g.reshape(1, d), w)


def _outproj_kernel(*refs, n_in):
    y_refs = refs[:n_in]
    w_ref, g_ref, h_ref, o_ref = refs[n_in:]
    acc = None
    off = 0
    for y_ref in y_refs:
        k = y_ref.shape[1]
        part = jnp.dot(y_ref[...], w_ref[off:off + k, :], preferred_element_type=F32)
        acc = part if acc is None else acc + part
        off += k
    o_ref[...] = h_ref[...] + _rms_f32(acc, g_ref[...])


def _outproj_res(ys, w, g, h, tm=512):
    bsz, seq, d = h.shape
    in_specs = [pl.BlockSpec((None, tm, y.shape[2]), lambda b, i: (b, i, 0)) for y in ys]
    in_specs += [
        pl.BlockSpec(w.shape, lambda b, i: (0, 0)),
        pl.BlockSpec((1, d), lambda b, i: (0, 0)),
        pl.BlockSpec((None, tm, d), lambda b, i: (b, i, 0)),
    ]
    return pl.pallas_call(
        functools.partial(_outproj_kernel, n_in=len(ys)),
        grid=(bsz, seq // tm),
        in_specs=in_specs,
        out_specs=pl.BlockSpec((None, tm, d), lambda b, i: (b, i, 0)),
        out_shape=jax.ShapeDtypeStruct(h.shape, F32),
        compiler_params=_cparams(("parallel", "parallel")),
        name="outproj_res",
    )(*ys, w, g.reshape(1, d), h)


def _mlp_ple_kernel(h_ref, g1_ref, w1_ref, w2_ref, g2_ref, p_ref, wg_ref, wp_ref, o_ref,
                    xn_ref, acc_ref):
    j = pl.program_id(1)

    @pl.when(j == 0)
    def _():
        xn_ref[...] = _rms_f32(h_ref[...], g1_ref[...]).astype(BF16)
        acc_ref[...] = jnp.zeros_like(acc_ref)

    a = jnp.dot(xn_ref[...], w1_ref[...], preferred_element_type=F32)
    a = jnp.square(jnp.maximum(a, 0.0)).astype(BF16)
    acc_ref[...] += jnp.dot(a, w2_ref[...], preferred_element_type=F32)

    @pl.when(j == pl.num_programs(1) - 1)
    def _():
        h = h_ref[...] + _rms_f32(acc_ref[...], g2_ref[...])
        gate = jax.nn.sigmoid(jnp.dot(h.astype(BF16), wg_ref[...], preferred_element_type=F32))
        emb = jnp.dot(p_ref[...].astype(BF16), wp_ref[...], preferred_element_type=F32)
        o_ref[...] = h + gate * emb


def _mlp_ple(h, g1, w1, w2, g2, p, wg, wp, tm=1024, tf=512):
    t, d = h.shape
    ff = w1.shape[1]
    pd = p.shape[1]
    return pl.pallas_call(
        _mlp_ple_kernel,
        grid=(t // tm, ff // tf),
        in_specs=[
            pl.BlockSpec((tm, d), lambda i, j: (i, 0)),
            pl.BlockSpec((1, d), lambda i, j: (0, 0)),
            pl.BlockSpec((d, tf), lambda i, j: (0, j)),
            pl.BlockSpec((tf, d), lambda i, j: (j, 0)),
            pl.BlockSpec((1, d), lambda i, j: (0, 0)),
            pl.BlockSpec((tm, pd), lambda i, j: (i, 0)),
            pl.BlockSpec((d, d), lambda i, j: (0, 0)),
            pl.BlockSpec((pd, d), lambda i, j: (0, 0)),
        ],
        out_specs=pl.BlockSpec((tm, d), lambda i, j: (i, 0)),
        out_shape=jax.ShapeDtypeStruct((t, d), F32),
        scratch_shapes=[pltpu.VMEM((tm, d), BF16), pltpu.VMEM((tm, d), F32)],
        compiler_params=_cparams(("parallel", "arbitrary")),
        name="mlp_ple",
    )(h, g1.reshape(1, d), w1, w2, g2.reshape(1, d), p, wg, wp)


def _flash_update(s_t, vt, m_ref, acc_ref):
    m_old = m_ref[...]
    m_new = jnp.maximum(m_old, jnp.max(s_t, axis=0, keepdims=True))
    alpha = jnp.exp2(m_old - m_new)
    p = jnp.exp2(s_t - m_new).astype(BF16)
    vt1 = jnp.concatenate([vt, jnp.ones((SUM_ROWS, vt.shape[1]), vt.dtype)], axis=0)
    acc_ref[...] = alpha * acc_ref[...] + jnp.dot(vt1, p, preferred_element_type=F32)
    m_ref[...] = m_new


def _flash_loop(lo, hi, score, vt_ref, m_ref, acc_ref):
    def body(j, carry):
        _flash_update(score(j), vt_ref[j], m_ref, acc_ref)
        return carry

    lax.fori_loop(lo, hi, body, 0)


def _flash_init(m_ref, acc_ref):
    m_ref[...] = jnp.full_like(m_ref, NEG)
    acc_ref[...] = jnp.zeros_like(acc_ref)


def _flash_out(acc_ref):
    dv = acc_ref.shape[0] - SUM_ROWS
    return acc_ref[0:dv, :] / acc_ref[dv:dv + 1, :]


def _pair_attn_kernel(q_ref, k_ref, vt_ref, bias_ref, lam_ref, gsub_ref, o_ref,
                      m_ref, acc_ref, *, windowed, diff, lam_init):
    i = pl.program_id(2)
    t = q_ref.shape[0]
    nb = bias_ref.shape[0]
    lane = lax.broadcasted_iota(I32, (t, LANES), 1)
    q = q_ref[...]
    zero = jnp.zeros_like(q)
    qs = jnp.concatenate([jnp.where(lane < HEAD_DIM, q, zero),
                          jnp.where(lane >= HEAD_DIM, q, zero)], axis=0)
    _flash_init(m_ref, acc_ref)

    def score(j):
        start = pl.multiple_of(j * t, t)
        bias = bias_ref[jnp.minimum(i - j, nb - 1)]
        return _dot_nt(k_ref[pl.ds(start, t), :], qs) + jnp.concatenate([bias, bias], axis=1)

    lo = jnp.maximum(i - (nb - 1), 0) if windowed else 0
    _flash_loop(lo, i + 1, score, vt_ref, m_ref, acc_ref)

    o_t = _flash_out(acc_ref)
    oa, ob = o_t[:, :t], o_t[:, t:]
    if diff:
        lp = lam_ref[...]
        lam = (jnp.exp(jnp.sum(lp[0:1] * lp[1:2], axis=-1, keepdims=True))
               - jnp.exp(jnp.sum(lp[2:3] * lp[3:4], axis=-1, keepdims=True)) + lam_init)
        d = oa - lam * ob
        y = d * lax.rsqrt(jnp.mean(d * d, axis=0, keepdims=True) + EPS) * gsub_ref[...]
        y = y * (1.0 - lam_init)
    else:
        row = lax.broadcasted_iota(I32, (LANES, t), 0)
        y = jnp.where(row < HEAD_DIM, oa, ob)
    o_ref[...] = y.T.astype(o_ref.dtype)


def _pair_attn(qk, vt, qcol, kcol, vblk, bias, lam_params, gsub, *, windowed, diff, lam_init):
    bsz, seq, _ = qk.shape
    t = ATT_T
    kern = functools.partial(_pair_attn_kernel, windowed=windowed, diff=diff, lam_init=lam_init)
    return pl.pallas_call(
        kern,
        grid=(bsz, PAIRS, seq // t),
        in_specs=[
            pl.BlockSpec((None, t, LANES), lambda b, h, i: (b, i, qcol + h)),
            pl.BlockSpec((None, seq, LANES), lambda b, h, i: (b, 0, kcol + h)),
            pl.BlockSpec((None, None, seq // t, LANES, t), lambda b, h, i: (b, vblk + h, 0, 0, 0)),
            pl.BlockSpec(bias.shape, lambda b, h, i: (0, 0, 0)),
            pl.BlockSpec(lam_params.shape, lambda b, h, i: (0, 0)),
            pl.BlockSpec(gsub.shape, lambda b, h, i: (0, 0)),
        ],
        out_specs=pl.BlockSpec((None, t, LANES), lambda b, h, i: (b, i, h)),
        out_shape=jax.ShapeDtypeStruct((bsz, seq, PAIRS * LANES), BF16),
        scratch_shapes=[pltpu.VMEM((1, 2 * t), F32),
                        pltpu.VMEM((LANES + SUM_ROWS, 2 * t), F32)],
        compiler_params=_cparams(("parallel", "parallel", "arbitrary")),
        name="diff_attn" if diff else "dilated_attn",
    )(qk, qk, vt, bias, lam_params, gsub)


def _causal_bias(t):
    r = jnp.arange(t)
    tri = jnp.where(r[None, :] >= r[:, None], 0.0, NEG).astype(F32)
    return jnp.stack([tri, jnp.zeros((t, t), F32)])


def _dilated_bias(t):
    max_dist = max(w for w, _ in B_BRANCHES)
    n_off = max_dist // t + 1
    r = jnp.arange(t)
    tiles = []
    for o in range(n_off):
        dist = o * t + r[None, :] - r[:, None]
        cnt = jnp.zeros((t, t), F32)
        for w, d in B_BRANCHES:
            cnt = cnt + ((dist >= 0) & (dist <= w) & (dist % d == 0)).astype(F32)
        tiles.append(jnp.where(cnt > 0, jnp.log2(jnp.maximum(cnt, 1.0)), NEG))
    return jnp.stack(tiles).astype(F32)


def _c_prep_kernel(pc_ref, gq_ref, gkv_ref, wq_ref, wk_ref, wvt_ref, wqi_ref,
                   ch_ref, sh_ref, cp_ref, sp_ref,
                   q_ref, k_ref, vt_ref, qi_ref, ki_ref, wit_ref):
    half = C_ROPE // 2
    cqn = _rms_f32(pc_ref[:, 0:C_Q_RANK], gq_ref[...]).astype(BF16)
    c0 = C_Q_RANK
    ckvn = _rms_f32(pc_ref[:, c0:c0 + C_KV_RANK], gkv_ref[...]).astype(BF16)
    c0 += C_KV_RANK
    kr = _rope_tile(pc_ref[:, c0:c0 + LANES], ch_ref[...], sh_ref[...], half)
    c0 += LANES
    ki_ref[...] = _rope_tile(pc_ref[:, c0:c0 + LANES], cp_ref[...], sp_ref[...], half).astype(BF16)
    c0 += LANES
    wit_ref[...] = (pc_ref[:, c0:c0 + LANES] * (IDX_HEADS ** -0.5)).T

    qi = jnp.dot(cqn, wqi_ref[...], preferred_element_type=F32)
    for c in range(0, IDX_HEADS * IDX_DIM, LANES):
        qi_ref[:, c:c + LANES] = _rope_tile(qi[:, c:c + LANES], cp_ref[...], sp_ref[...],
                                            half).astype(BF16)

    for h in range(C_HEADS):
        cols = slice(h * LANES, (h + 1) * LANES)
        qh = jnp.dot(cqn, wq_ref[:, cols], preferred_element_type=F32)
        q_ref[:, cols] = (_rope_tile(qh, ch_ref[...], sh_ref[...], half)
                          * (C_SCALE * LOG2E)).astype(BF16)
        kh = jnp.dot(ckvn, wk_ref[:, cols], preferred_element_type=F32)
        k_ref[:, cols] = (kh + kr).astype(BF16)
    for hp in range(C_HEADS // 2):
        vt_ref[hp] = _dot_nt(wvt_ref[hp * LANES:(hp + 1) * LANES, :], ckvn).astype(BF16)


def _c_prep(pc, gq, gkv, wq, wk, wvt, wqi, ch, sh, cp, sp):
    bsz, seq, _ = pc.shape
    tm = DSA_T
    hw = C_HEADS * LANES
    row = lambda b, i: (b, i, 0)
    tab = pl.BlockSpec((tm, LANES), lambda b, i: (i, 0))
    full2 = lambda a: pl.BlockSpec(a.shape, lambda b, i: (0, 0))
    return pl.pallas_call(
        _c_prep_kernel,
        grid=(bsz, seq // tm),
        in_specs=[
            pl.BlockSpec((None, tm, C_PAD), row),
            full2(gq), full2(gkv), full2(wq), full2(wk), full2(wvt), full2(wqi),
            tab, tab, tab, tab,
        ],
        out_specs=[
            pl.BlockSpec((None, tm, hw), row),
            pl.BlockSpec((None, tm, hw), row),
            pl.BlockSpec((None, C_HEADS // 2, None, LANES, tm), lambda b, i: (b, 0, i, 0, 0)),
            pl.BlockSpec((None, tm, IDX_HEADS * IDX_DIM), row),
            pl.BlockSpec((None, tm, LANES), row),
            pl.BlockSpec((None, LANES, tm), lambda b, i: (b, 0, i)),
        ],
        out_shape=[
            jax.ShapeDtypeStruct((bsz, seq, hw), BF16),
            jax.ShapeDtypeStruct((bsz, seq, hw), BF16),
            jax.ShapeDtypeStruct((bsz, C_HEADS // 2, seq // tm, LANES, tm), BF16),
            jax.ShapeDtypeStruct((bsz, seq, IDX_HEADS * IDX_DIM), BF16),
            jax.ShapeDtypeStruct((bsz, seq, LANES), BF16),
            jax.ShapeDtypeStruct((bsz, LANES, seq), F32),
        ],
        compiler_params=_cparams(("parallel", "parallel")),
        name="c_prep",
    )(pc, gq, gkv, wq, wk, wvt, wqi, ch, sh, cp, sp)


def _sort_key(x):
    b = pltpu.bitcast(x, I32)
    return b ^ (lax.shift_right_arithmetic(b, 31) & 0x7FFFFFFF)


def _dsa_select(iq, qi_ref, ki_ref, wit_ref, qm_ref, key_ref, bias_ref, pfx_ref, *,
                k_sel, idx_bits):
    t = DSA_T
    nch = iq + 1
    q_pos = iq * t + lax.broadcasted_iota(I32, (t, t), 1)
    k_in = lax.broadcasted_iota(I32, (t, t), 0)

    lane = lax.broadcasted_iota(I32, (t, LANES), 1)
    for h in range(IDX_HEADS):
        tile = qi_ref[:, (h // 2) * LANES:(h // 2 + 1) * LANES]
        keep = (lane < IDX_DIM) if h % 2 == 0 else (lane >= IDX_DIM)
        qm_ref[h] = jnp.where(keep, tile, jnp.zeros_like(tile))

    def score_body(c, carry):
        start = pl.multiple_of(c * t, t)
        kic = ki_ref[pl.ds(start, t), :]
        sc = jnp.zeros((t, t), F32)
        for h in range(IDX_HEADS):
            sc = sc + jnp.maximum(_dot_nt(kic, qm_ref[h]), 0.0) * wit_ref[h:h + 1, :]
        sc = sc * (IDX_DIM ** -0.5)
        causal = (c * t + k_in) <= q_pos
        key_ref[c] = jnp.where(causal, _sort_key(sc), INT_MIN)
        return carry

    lax.fori_loop(0, nch, score_body, 0)

    def count(pred):
        def body(c, cnt):
            hit = jnp.where(pred(key_ref[c], c), 1, 0)
            return cnt + jnp.sum(hit.reshape(t // SUBLANES, SUBLANES, t), axis=0)
        cnt = lax.fori_loop(0, nch, body, jnp.zeros((SUBLANES, t), I32))
        return jnp.sum(cnt, axis=0, keepdims=True)

    def bit_body(it, carry):
        thr, n_thr = carry
        cand = thr ^ lax.shift_left(jnp.int32(1), 31 - it)
        n_ge = count(lambda key, c: key >= cand)
        ok = n_ge >= k_sel
        return jnp.where(ok, cand, thr), jnp.where(ok, n_ge, n_thr)

    thr, n_thr = lax.fori_loop(0, 32, bit_body, (jnp.full((1, t), INT_MIN, I32),
                                                 jnp.zeros((1, t), I32)))
    thr = jnp.maximum(thr, INT_MIN + 1)
    pfx_ref[...] = jnp.full_like(pfx_ref, 2 ** 30)

    @pl.when(jnp.max(n_thr) > k_sel)
    def _():
        need = k_sel - count(lambda key, c: key > thr)

        def pbit(it, pfx):
            cand = pfx + lax.shift_left(jnp.int32(1), idx_bits - 1 - it)
            n = count(lambda key, c: jnp.logical_and(key == thr, (c * t + k_in) < cand))
            return jnp.where(n < need, cand, pfx)
        pfx_ref[...] = lax.fori_loop(0, idx_bits, pbit, jnp.zeros((1, t), I32))

    pfx = pfx_ref[...]

    def bias_body(c, carry):
        key = key_ref[c]
        tie = jnp.where((c * t + k_in) <= pfx, 0.0, NEG)
        bias_ref[c] = jnp.where(key > thr, 0.0, jnp.where(key == thr, tie, NEG))
        return carry

    lax.fori_loop(0, nch, bias_body, 0)


def _dsa_kernel(q_ref, k_ref, vt_ref, qi_ref, ki_ref, wit_ref, o_ref,
                qm_ref, key_ref, bias_ref, pfx_ref, m_ref, acc_ref, *, k_sel, idx_bits):
    iq = pl.program_id(1)
    t = DSA_T

    @pl.when(pl.program_id(2) == 0)
    def _():
        _dsa_select(iq, qi_ref, ki_ref, wit_ref, qm_ref, key_ref, bias_ref, pfx_ref,
                    k_sel=k_sel, idx_bits=idx_bits)

    _flash_init(m_ref, acc_ref)

    def score(c):
        start = pl.multiple_of(c * t, t)
        bias = bias_ref[c]
        s0 = _dot_nt(k_ref[pl.ds(start, t), 0:LANES], q_ref[:, 0:LANES]) + bias
        s1 = _dot_nt(k_ref[pl.ds(start, t), LANES:2 * LANES], q_ref[:, LANES:2 * LANES]) + bias
        return jnp.concatenate([s0, s1], axis=1)

    _flash_loop(0, iq + 1, score, vt_ref, m_ref, acc_ref)

    o_t = _flash_out(acc_ref)
    row = lax.broadcasted_iota(I32, (LANES, t), 0)
    o_ref[...] = jnp.where(row < C_VDIM, o_t[:, :t], o_t[:, t:]).T.astype(o_ref.dtype)


def _dsa(q, k, vt, qi, ki, wit):
    bsz, seq, _ = q.shape
    t = DSA_T
    nq = seq // t
    k_sel = min(TOPK_MAX, seq // 4)
    idx_bits = max(1, (seq - 1).bit_length())
    kern = functools.partial(_dsa_kernel, k_sel=k_sel, idx_bits=idx_bits)
    return pl.pallas_call(
        kern,
        grid=(bsz, nq, C_HEADS // 2),
        in_specs=[
            pl.BlockSpec((None, t, 2 * LANES), lambda b, i, h: (b, i, h)),
            pl.BlockSpec((None, seq, 2 * LANES), lambda b, i, h: (b, 0, h)),
            pl.BlockSpec((None, None, nq, LANES, t), lambda b, i, h: (b, h, 0, 0, 0)),
            pl.BlockSpec((None, t, IDX_HEADS * IDX_DIM), lambda b, i, h: (b, i, 0)),
            pl.BlockSpec((None, seq, LANES), lambda b, i, h: (b, 0, 0)),
            pl.BlockSpec((None, LANES, t), lambda b, i, h: (b, 0, i)),
        ],
        out_specs=pl.BlockSpec((None, t, LANES), lambda b, i, h: (b, i, h)),
        out_shape=jax.ShapeDtypeStruct((bsz, seq, C_HEADS * C_VDIM), BF16),
        scratch_shapes=[
            pltpu.VMEM((IDX_HEADS, t, LANES), BF16),
            pltpu.VMEM((nq, t, t), I32),
            pltpu.VMEM((nq, t, t), F32),
            pltpu.VMEM((1, t), I32),
            pltpu.VMEM((1, 2 * t), F32),
            pltpu.VMEM((LANES + SUM_ROWS, 2 * t), F32),
        ],
        compiler_params=_cparams(("parallel", "arbitrary", "arbitrary")),
        name="dsa",
    )(q, k, vt, qi, ki, wit)


def _prep_c_weights(w_in_c, w_uq, w_uk, w_uv):
    d = w_in_c.shape[0]
    cq_w, ckv_w, kr_w, kidx_w, widx_w = jnp.split(
        w_in_c, [C_Q_RANK, C_Q_RANK + C_KV_RANK, C_Q_RANK + C_KV_RANK + C_ROPE,
                 C_Q_RANK + C_KV_RANK + C_ROPE + IDX_DIM], axis=1)
    zeros = lambda n: jnp.zeros((d, n), w_in_c.dtype)
    w_c = jnp.concatenate([cq_w, ckv_w, zeros(C_NOPE), kr_w, zeros(LANES - C_NOPE - C_ROPE),
                           kidx_w, kidx_w, widx_w, zeros(LANES - IDX_HEADS)], axis=1)
    pad = LANES - C_NOPE - C_ROPE
    uq = w_uq.reshape(C_Q_RANK, C_HEADS, C_NOPE + C_ROPE)
    w_q = jnp.pad(uq, ((0, 0), (0, 0), (0, pad))).reshape(C_Q_RANK, C_HEADS * LANES)
    w_k = jnp.pad(w_uk, ((0, 0), (0, 0), (0, LANES - C_NOPE))).reshape(C_KV_RANK, C_HEADS * LANES)
    w_vt = w_uv.reshape(C_KV_RANK, C_HEADS * C_VDIM).T
    return w_c.astype(BF16), w_q.astype(BF16), w_k.astype(BF16), w_vt.astype(BF16)


def _mixer_ab(h, g_pre, w_in, w_out, g_post, lam_params, g_sub, lam_init, tabs):
    cos_a, sin_a = tabs
    aq, ak, av, bq, bk, bv = jnp.split(w_in, [512, 1024, 1536, 2048, 2560], axis=1)
    w_qk = jnp.concatenate([aq, ak, bq, bk], axis=1).astype(BF16)
    w_vt = jnp.concatenate([av, bv], axis=1).T.astype(BF16)
    qk, vt = _proj_ab(h, g_pre, w_qk, w_vt, cos_a, sin_a, HEAD_DIM // 2, ATT_T,
                      q_groups=(0, 1, 2, 3, 8, 9, 10, 11), q_scale=HEAD_DIM ** -0.5 * LOG2E)
    gsub_col = g_sub.reshape(A_VDIM, 1)
    ao = _pair_attn(qk, vt, 0, 4, 0, _causal_bias(ATT_T), lam_params, gsub_col,
                    windowed=False, diff=True, lam_init=lam_init)
    bo = _pair_attn(qk, vt, 8, 12, 4, _dilated_bias(ATT_T), lam_params, gsub_col,
                    windowed=True, diff=False, lam_init=lam_init)
    return _outproj_res([ao, bo], w_out.astype(BF16), g_post, h)


def _mixer_c(h, g_pre, w_in_c, g_cq, g_ckv, w_uq, w_qi, w_uk, w_uv, w_out, g_post, tabs):
    cos_h, sin_h, cos_p, sin_p = tabs
    w_c, w_q, w_k, w_vt = _prep_c_weights(w_in_c, w_uq, w_uk, w_uv)
    pc = _proj_c(h, g_pre, w_c)
    q, k, vt, qi, ki, wit = _c_prep(pc, g_cq.reshape(1, -1), g_ckv.reshape(1, -1), w_q, w_k, w_vt,
                                    w_qi.astype(BF16), cos_h, sin_h, cos_p, sin_p)
    o = _dsa(q, k, vt, qi, ki, wit)
    return _outproj_res([o], w_out.astype(BF16), g_post, h)


def kernel(x, p, g_mix_pre, g_mix_post, g_mlp_pre, g_mlp_post, w_mlp_in, w_mlp_out, w_ple_proj, w_ple_gate, w_in_ab, w_out_ab, diff_lq1, diff_lk1, diff_lq2, diff_lk2, g_diff_sub, w_in_c, g_cq, g_ckv, w_uq, w_qi, w_uk, w_uv, w_out_c):
    bsz, seq, d = x.shape
    depth = p.shape[0]
    tabs_a = _rope_tables(seq, HEAD_DIM // 2, HEAD_DIM, 0, HEAD_DIM)
    tabs_c = (_rope_tables(seq, C_ROPE // 2, LANES, C_NOPE, C_ROPE)
              + _rope_tables(seq, IDX_ROPE // 2, IDX_DIM, 0, IDX_ROPE))
    h = x
    for i in range(depth):
        j = i // 2
        if i % 2 == 0:
            lam_init = 0.8 - 0.6 * math.exp(-0.3 * i)
            lam_params = jnp.stack([diff_lq1[j], diff_lk1[j], diff_lq2[j], diff_lk2[j]])
            h = _mixer_ab(h, g_mix_pre[i], w_in_ab[j], w_out_ab[j], g_mix_post[i], lam_params,
                          g_diff_sub[j], lam_init, tabs_a)
        else:
            h = _mixer_c(h, g_mix_pre[i], w_in_c[j], g_cq[j], g_ckv[j], w_uq[j], w_qi[j],
                         w_uk[j], w_uv[j], w_out_c[j], g_mix_post[i], tabs_c)
        hf = _mlp_ple(h.reshape(bsz * seq, d), g_mlp_pre[i], w_mlp_in[i].astype(BF16),
                      w_mlp_out[i].astype(BF16), g_mlp_post[i], p[i].reshape(bsz * seq, -1),
                      w_ple_gate[i].astype(BF16), w_ple_proj[i].astype(BF16))
        h = hf.reshape(bsz, seq, d)
    return h
```

```python
import functools
import math

import jax
import jax.numpy as jnp
from jax import lax
from jax.experimental import pallas as pl
from jax.experimental.pallas import tpu as pltpu

F32 = jnp.float32
BF16 = jnp.bfloat16
I32 = jnp.int32

LANES = 128
SUBLANES = 8
VMEM_LIMIT = 56 * 1024 * 1024

D_MODEL = 1024
HEAD_DIM = 64
A_HEADS = 4
A_VDIM = 2 * HEAD_DIM
B_HEADS = 8
B_BRANCHES = ((128, 1), (512, 4), (2048, 16))
C_HEADS = 16
C_Q_RANK = 384
C_KV_RANK = 256
C_NOPE = 64
C_ROPE = 32
C_VDIM = 64
C_SCALE = (C_NOPE + C_ROPE) ** -0.5
IDX_HEADS = 8
IDX_DIM = 64
IDX_ROPE = 32
TOPK_MAX = 256
D_FF = 4 * D_MODEL
PLE_DIM = 256
ROPE_THETA = 10000.0
EPS = 1e-6
NEG = -1e30
LOG2E = 1.4426950408889634
SUM_ROWS = 16
FLASH_UNROLL = 4
INT_MIN = -2 ** 31

C_PAD = 1024
DSA_T = 512
ATT_T = 512
PAIRS = 4


def _cparams(sem):
    return pltpu.CompilerParams(dimension_semantics=sem, vmem_limit_bytes=VMEM_LIMIT)


def _rms_f32(x, g):
    return x * lax.rsqrt(jnp.mean(x * x, axis=-1, keepdims=True) + EPS) * g


def _dot_nt(a, b):
    return lax.dot_general(a, b, (((1,), (1,)), ((), ())), preferred_element_type=F32)


def _rope_tile(y, cos, sin):
    return y * cos + pltpu.roll(y, LANES // 2, 1) * sin


def _rope_tables(seq, half, group):
    inv = ROPE_THETA ** (-jnp.arange(half, dtype=F32) / half)
    ang = jnp.arange(seq).astype(F32)[:, None] * inv[None, :]
    lane = jnp.arange(LANES)
    idx = lane % half
    on = (lane % group < half)[None, :]
    cos = jnp.where(on, jnp.cos(ang)[:, idx], 1.0)
    sgn = jnp.where(lane < LANES // 2, -1.0, 1.0)[None, :]
    sin = jnp.where(on, jnp.sin(ang)[:, idx] * sgn, 0.0)
    return cos.astype(F32), sin.astype(F32)


def _pair_perm(unit, half):
    rest = (unit - 2 * half) // 2
    order = []
    for part in range(2):
        for head in range(2):
            base = head * unit
            order += list(range(base + part * half, base + (part + 1) * half))
            order += list(range(base + 2 * half + part * rest, base + 2 * half + (part + 1) * rest))
    return jnp.asarray(order, dtype=I32)


def _permute_tiles(w, perm):
    rows, cols = w.shape
    return w.reshape(rows, cols // LANES, LANES)[:, :, perm].reshape(rows, cols)


def _proj_ab_kernel(x_ref, g_ref, w_ref, wvt_ref, cos_ref, sin_ref, o_ref, vt_ref, *,
                    q_groups, q_scale):
    xn = _rms_f32(x_ref[...], g_ref[...]).astype(BF16)
    tm = xn.shape[0]
    n = w_ref.shape[1]
    slab = 512
    for s0 in range(0, n, slab):
        y = jnp.dot(xn, w_ref[:, s0:s0 + slab], preferred_element_type=F32)
        for c0 in range(0, slab, LANES):
            yc = _rope_tile(y[:, c0:c0 + LANES], cos_ref[...], sin_ref[...])
            if (s0 + c0) // LANES in q_groups:
                yc = yc * q_scale
            o_ref[:, s0 + c0:s0 + c0 + LANES] = yc.astype(o_ref.dtype)
    tk = vt_ref.shape[-1]
    for hb in range(vt_ref.shape[0]):
        vt = _dot_nt(wvt_ref[hb * LANES:(hb + 1) * LANES, :], xn).astype(vt_ref.dtype)
        for s in range(tm // tk):
            vt_ref[hb, s] = vt[:, s * tk:(s + 1) * tk]


def _proj_ab(x, g, w_qk, w_vt, cos, sin, tk, q_groups, q_scale, tm=512):
    bsz, seq, d = x.shape
    n = w_qk.shape[1]
    nvb = w_vt.shape[0] // LANES
    return pl.pallas_call(
        functools.partial(_proj_ab_kernel, q_groups=frozenset(q_groups), q_scale=q_scale),
        grid=(bsz, seq // tm),
        in_specs=[
            pl.BlockSpec((None, tm, d), lambda b, i: (b, i, 0)),
            pl.BlockSpec((1, d), lambda b, i: (0, 0)),
            pl.BlockSpec((d, n), lambda b, i: (0, 0)),
            pl.BlockSpec(w_vt.shape, lambda b, i: (0, 0)),
            pl.BlockSpec((tm, LANES), lambda b, i: (i, 0)),
            pl.BlockSpec((tm, LANES), lambda b, i: (i, 0)),
        ],
        out_specs=[
            pl.BlockSpec((None, tm, n), lambda b, i: (b, i, 0)),
            pl.BlockSpec((None, nvb, tm // tk, LANES, tk), lambda b, i: (b, 0, i, 0, 0)),
        ],
        out_shape=[
            jax.ShapeDtypeStruct((bsz, seq, n), BF16),
            jax.ShapeDtypeStruct((bsz, nvb, seq // tk, LANES, tk), BF16),
        ],
        compiler_params=_cparams(("parallel", "parallel")),
        name="proj_ab",
    )(x, g.reshape(1, d), w_qk, w_vt, cos, sin)


def _proj_c_kernel(x_ref, g_ref, w_ref, o_ref):
    xn = _rms_f32(x_ref[...], g_ref[...]).astype(BF16)
    o_ref[...] = jnp.dot(xn, w_ref[...], preferred_element_type=F32)


def _proj_c(x, g, w, tm=512):
    bsz, seq, d = x.shape
    n = w.shape[1]
    return pl.pallas_call(
        _proj_c_kernel,
        grid=(bsz, seq // tm),
        in_specs=[
            pl.BlockSpec((None, tm, d), lambda b, i: (b, i, 0)),
            pl.BlockSpec((1, d), lambda b, i: (0, 0)),
            pl.BlockSpec((d, n), lambda b, i: (0, 0)),
        ],
        out_specs=pl.BlockSpec((None, tm, n), lambda b, i: (b, i, 0)),
        out_shape=jax.ShapeDtypeStruct((bsz, seq, n), F32),
        compiler_params=_cparams(("parallel", "parallel")),
        name="proj_c",
    )(x, g.reshape(1, d), w)


def _outproj_kernel(*refs, n_in):
    y_refs = refs[:n_in]
    w_ref, g_ref, h_ref, o_ref = refs[n_in:]
    acc = None
    off = 0
    for y_ref in y_refs:
        k = y_ref.shape[1]
        part = jnp.dot(y_ref[...], w_ref[off:off + k, :], preferred_element_type=F32)
        acc = part if acc is None else acc + part
        off += k
    o_ref[...] = h_ref[...] + _rms_f32(acc, g_ref[...])


def _outproj_res(ys, w, g, h, tm=512):
    bsz, seq, d = h.shape
    in_specs = [pl.BlockSpec((None, tm, y.shape[2]), lambda b, i: (b, i, 0)) for y in ys]
    in_specs += [
        pl.BlockSpec(w.shape, lambda b, i: (0, 0)),
        pl.BlockSpec((1, d), lambda b, i: (0, 0)),
        pl.BlockSpec((None, tm, d), lambda b, i: (b, i, 0)),
    ]
    return pl.pallas_call(
        functools.partial(_outproj_kernel, n_in=len(ys)),
        grid=(bsz, seq // tm),
        in_specs=in_specs,
        out_specs=pl.BlockSpec((None, tm, d), lambda b, i: (b, i, 0)),
        out_shape=jax.ShapeDtypeStruct(h.shape, F32),
        compiler_params=_cparams(("parallel", "parallel")),
        name="outproj_res",
    )(*ys, w, g.reshape(1, d), h)


def _mlp_ple_kernel(h_ref, g1_ref, w1_ref, w2_ref, g2_ref, p_ref, wg_ref, wp_ref, o_ref,
                    xn_ref, acc_ref):
    j = pl.program_id(1)

    @pl.when(j == 0)
    def _():
        xn_ref[...] = _rms_f32(h_ref[...], g1_ref[...]).astype(BF16)
        acc_ref[...] = jnp.zeros_like(acc_ref)

    a = jnp.dot(xn_ref[...], w1_ref[...], preferred_element_type=F32)
    a = jnp.square(jnp.maximum(a, 0.0)).astype(BF16)
    acc_ref[...] += jnp.dot(a, w2_ref[...], preferred_element_type=F32)

    @pl.when(j == pl.num_programs(1) - 1)
    def _():
        h = h_ref[...] + _rms_f32(acc_ref[...], g2_ref[...])
        gate = jax.nn.sigmoid(jnp.dot(h.astype(BF16), wg_ref[...], preferred_element_type=F32))
        emb = jnp.dot(p_ref[...].astype(BF16), wp_ref[...], preferred_element_type=F32)
        o_ref[...] = h + gate * emb


def _mlp_ple(h, g1, w1, w2, g2, p, wg, wp, tm=1024, tf=512):
    t, d = h.shape
    ff = w1.shape[1]
    pd = p.shape[1]
    return pl.pallas_call(
        _mlp_ple_kernel,
        grid=(t // tm, ff // tf),
        in_specs=[
            pl.BlockSpec((tm, d), lambda i, j: (i, 0)),
            pl.BlockSpec((1, d), lambda i, j: (0, 0)),
            pl.BlockSpec((d, tf), lambda i, j: (0, j)),
            pl.BlockSpec((tf, d), lambda i, j: (j, 0)),
            pl.BlockSpec((1, d), lambda i, j: (0, 0)),
            pl.BlockSpec((tm, pd), lambda i, j: (i, 0)),
            pl.BlockSpec((d, d), lambda i, j: (0, 0)),
            pl.BlockSpec((pd, d), lambda i, j: (0, 0)),
        ],
        out_specs=pl.BlockSpec((tm, d), lambda i, j: (i, 0)),
        out_shape=jax.ShapeDtypeStruct((t, d), F32),
        scratch_shapes=[pltpu.VMEM((tm, d), BF16), pltpu.VMEM((tm, d), F32)],
        compiler_params=_cparams(("parallel", "arbitrary")),
        name="mlp_ple",
    )(h, g1.reshape(1, d), w1, w2, g2.reshape(1, d), p, wg, wp)


def _flash_update(s_t, vt, m_ref, acc_ref):
    m_old = m_ref[...]
    m_new = jnp.maximum(m_old, jnp.max(s_t, axis=0, keepdims=True))
    alpha = jnp.exp2(m_old - m_new)
    p = jnp.exp2(s_t - m_new).astype(BF16)
    vt1 = jnp.concatenate([vt, jnp.ones((SUM_ROWS, vt.shape[1]), vt.dtype)], axis=0)
    acc_ref[...] = alpha * acc_ref[...] + jnp.dot(vt1, p, preferred_element_type=F32)
    m_ref[...] = m_new


def _flash_loop(lo, hi, score, vt_ref, m_ref, acc_ref):
    start = lo
    width = FLASH_UNROLL
    while width >= 1:
        trips = (hi - start) // width

        def body(jj, carry, start=start, width=width):
            j0 = start + width * jj
            scores = [score(j0 + u) for u in range(width)]
            for u in range(width):
                _flash_update(scores[u], vt_ref[j0 + u], m_ref, acc_ref)
            return carry

        lax.fori_loop(0, trips, body, 0)
        start = start + width * trips
        width //= 2


def _flash_init(m_ref, acc_ref):
    m_ref[...] = jnp.full_like(m_ref, NEG)
    acc_ref[...] = jnp.zeros_like(acc_ref)


def _flash_out(acc_ref):
    dv = acc_ref.shape[0] - SUM_ROWS
    return acc_ref[0:dv, :] / acc_ref[dv:dv + 1, :]


def _pair_attn_kernel(q_ref, k_ref, vt_ref, bias_ref, lam_ref, gsub_ref, o_ref,
                      m_ref, acc_ref, *, windowed, diff, lam_init):
    i = pl.program_id(2)
    t = q_ref.shape[0]
    nb = bias_ref.shape[0]
    lane = lax.broadcasted_iota(I32, (t, LANES), 1)
    q = q_ref[...]
    zero = jnp.zeros_like(q)
    first = (lane % HEAD_DIM) < HEAD_DIM // 2
    qs = jnp.concatenate([jnp.where(first, q, zero), jnp.where(first, zero, q)], axis=0)
    _flash_init(m_ref, acc_ref)

    def score(j):
        start = pl.multiple_of(j * t, t)
        bias = bias_ref[jnp.minimum(i - j, nb - 1)]
        return _dot_nt(k_ref[pl.ds(start, t), :], qs) + jnp.concatenate([bias, bias], axis=1)

    lo = jnp.maximum(i - (nb - 1), 0) if windowed else 0
    _flash_loop(lo, i + 1, score, vt_ref, m_ref, acc_ref)

    o_t = _flash_out(acc_ref)
    oa, ob = o_t[:, :t], o_t[:, t:]
    if diff:
        lp = lam_ref[...]
        lam = (jnp.exp(jnp.sum(lp[0:1] * lp[1:2], axis=-1, keepdims=True))
               - jnp.exp(jnp.sum(lp[2:3] * lp[3:4], axis=-1, keepdims=True)) + lam_init)
        d = oa - lam * ob
        y = d * lax.rsqrt(jnp.mean(d * d, axis=0, keepdims=True) + EPS) * gsub_ref[...]
        y = y * (1.0 - lam_init)
    else:
        row = lax.broadcasted_iota(I32, (LANES, t), 0)
        y = jnp.where(row < HEAD_DIM, oa, ob)
    o_ref[...] = y.T.astype(o_ref.dtype)


def _pair_attn(qk, vt, qcol, kcol, vblk, bias, lam_params, gsub, *, windowed, diff, lam_init):
    bsz, seq, _ = qk.shape
    t = ATT_T
    kern = functools.partial(_pair_attn_kernel, windowed=windowed, diff=diff, lam_init=lam_init)
    return pl.pallas_call(
        kern,
        grid=(bsz, PAIRS, seq // t),
        in_specs=[
            pl.BlockSpec((None, t, LANES), lambda b, h, i: (b, i, qcol + h)),
            pl.BlockSpec((None, seq, LANES), lambda b, h, i: (b, 0, kcol + h)),
            pl.BlockSpec((None, None, seq // t, LANES, t), lambda b, h, i: (b, vblk + h, 0, 0, 0)),
            pl.BlockSpec(bias.shape, lambda b, h, i: (0, 0, 0)),
            pl.BlockSpec(lam_params.shape, lambda b, h, i: (0, 0)),
            pl.BlockSpec(gsub.shape, lambda b, h, i: (0, 0)),
        ],
        out_specs=pl.BlockSpec((None, t, LANES), lambda b, h, i: (b, i, h)),
        out_shape=jax.ShapeDtypeStruct((bsz, seq, PAIRS * LANES), BF16),
        scratch_shapes=[pltpu.VMEM((1, 2 * t), F32),
                        pltpu.VMEM((LANES + SUM_ROWS, 2 * t), F32)],
        compiler_params=_cparams(("parallel", "parallel", "arbitrary")),
        name="diff_attn" if diff else "dilated_attn",
    )(qk, qk, vt, bias, lam_params, gsub)


def _causal_bias(t):
    r = jnp.arange(t)
    tri = jnp.where(r[None, :] >= r[:, None], 0.0, NEG).astype(F32)
    return jnp.stack([tri, jnp.zeros((t, t), F32)])


def _dilated_bias(t):
    max_dist = max(w for w, _ in B_BRANCHES)
    n_off = max_dist // t + 1
    r = jnp.arange(t)
    tiles = []
    for o in range(n_off):
        dist = o * t + r[None, :] - r[:, None]
        cnt = jnp.zeros((t, t), F32)
        for w, d in B_BRANCHES:
            cnt = cnt + ((dist >= 0) & (dist <= w) & (dist % d == 0)).astype(F32)
        tiles.append(jnp.where(cnt > 0, jnp.log2(jnp.maximum(cnt, 1.0)), NEG))
    return jnp.stack(tiles).astype(F32)


def _c_prep_kernel(pc_ref, gq_ref, gkv_ref, wq_ref, wk_ref, wvt_ref, wqi_ref,
                   ch_ref, sh_ref, cp_ref, sp_ref,
                   q_ref, k_ref, vt_ref, qi_ref, ki_ref, wit_ref):
    cqn = _rms_f32(pc_ref[:, 0:C_Q_RANK], gq_ref[...]).astype(BF16)
    c0 = C_Q_RANK
    ckvn = _rms_f32(pc_ref[:, c0:c0 + C_KV_RANK], gkv_ref[...]).astype(BF16)
    c0 += C_KV_RANK
    kr = _rope_tile(pc_ref[:, c0:c0 + LANES], ch_ref[...], sh_ref[...])
    c0 += LANES
    ki_ref[...] = _rope_tile(pc_ref[:, c0:c0 + LANES], cp_ref[...], sp_ref[...]).astype(BF16)
    c0 += LANES
    wit_ref[...] = (pc_ref[:, c0:c0 + LANES] * (IDX_HEADS ** -0.5)).T

    qi = jnp.dot(cqn, wqi_ref[...], preferred_element_type=F32)
    for c in range(0, IDX_HEADS * IDX_DIM, LANES):
        qi_ref[:, c:c + LANES] = _rope_tile(qi[:, c:c + LANES], cp_ref[...],
                                            sp_ref[...]).astype(BF16)

    for hp in range(C_HEADS // 2):
        pair = slice(2 * hp * LANES, 2 * (hp + 1) * LANES)
        q2 = jnp.dot(cqn, wq_ref[:, pair], preferred_element_type=F32)
        k2 = jnp.dot(ckvn, wk_ref[:, pair], preferred_element_type=F32)
        for e in range(2):
            cols = slice((2 * hp + e) * LANES, (2 * hp + e + 1) * LANES)
            qh = q2[:, e * LANES:(e + 1) * LANES]
            q_ref[:, cols] = (_rope_tile(qh, ch_ref[...], sh_ref[...])
                              * (C_SCALE * LOG2E)).astype(BF16)
            k_ref[:, cols] = (k2[:, e * LANES:(e + 1) * LANES] + kr).astype(BF16)
    for hp in range(C_HEADS // 2):
        vt_ref[hp] = _dot_nt(wvt_ref[hp * LANES:(hp + 1) * LANES, :], ckvn).astype(BF16)


def _c_prep(pc, gq, gkv, wq, wk, wvt, wqi, ch, sh, cp, sp):
    bsz, seq, _ = pc.shape
    tm = DSA_T
    hw = C_HEADS * LANES
    row = lambda b, i: (b, i, 0)
    tab = pl.BlockSpec((tm, LANES), lambda b, i: (i, 0))
    full2 = lambda a: pl.BlockSpec(a.shape, lambda b, i: (0, 0))
    return pl.pallas_call(
        _c_prep_kernel,
        grid=(bsz, seq // tm),
        in_specs=[
            pl.BlockSpec((None, tm, C_PAD), row),
            full2(gq), full2(gkv), full2(wq), full2(wk), full2(wvt), full2(wqi),
            tab, tab, tab, tab,
        ],
        out_specs=[
            pl.BlockSpec((None, tm, hw), row),
            pl.BlockSpec((None, tm, hw), row),
            pl.BlockSpec((None, C_HEADS // 2, None, LANES, tm), lambda b, i: (b, 0, i, 0, 0)),
            pl.BlockSpec((None, tm, IDX_HEADS * IDX_DIM), row),
            pl.BlockSpec((None, tm, LANES), row),
            pl.BlockSpec((None, LANES, tm), lambda b, i: (b, 0, i)),
        ],
        out_shape=[
            jax.ShapeDtypeStruct((bsz, seq, hw), BF16),
            jax.ShapeDtypeStruct((bsz, seq, hw), BF16),
            jax.ShapeDtypeStruct((bsz, C_HEADS // 2, seq // tm, LANES, tm), BF16),
            jax.ShapeDtypeStruct((bsz, seq, IDX_HEADS * IDX_DIM), BF16),
            jax.ShapeDtypeStruct((bsz, seq, LANES), BF16),
            jax.ShapeDtypeStruct((bsz, LANES, seq), F32),
        ],
        compiler_params=_cparams(("parallel", "parallel")),
        name="c_prep",
    )(pc, gq, gkv, wq, wk, wvt, wqi, ch, sh, cp, sp)


def _sort_key(x):
    b = pltpu.bitcast(x, I32)
    return b ^ (lax.shift_right_arithmetic(b, 31) & 0x7FFFFFFF)


def _dsa_select(iq, qi_ref, ki_ref, wit_ref, qm_ref, key_ref, bias_ref, pfx_ref, *,
                k_sel, idx_bits):
    t = DSA_T
    nch = iq + 1
    q_pos = iq * t + lax.broadcasted_iota(I32, (t, t), 1)
    k_in = lax.broadcasted_iota(I32, (t, t), 0)

    lane = lax.broadcasted_iota(I32, (t, LANES), 1)
    for h in range(IDX_HEADS):
        tile = qi_ref[:, (h // 2) * LANES:(h // 2 + 1) * LANES]
        even = (lane % IDX_DIM) < IDX_DIM // 2
        keep = even if h % 2 == 0 else jnp.logical_not(even)
        qm_ref[h] = jnp.where(keep, tile, jnp.zeros_like(tile))

    def score_body(c, carry):
        start = pl.multiple_of(c * t, t)
        kic = ki_ref[pl.ds(start, t), :]
        sc = jnp.zeros((t, t), F32)
        for h in range(IDX_HEADS):
            sc = sc + jnp.maximum(_dot_nt(kic, qm_ref[h]), 0.0) * wit_ref[h:h + 1, :]
        sc = sc * (IDX_DIM ** -0.5)
        causal = (c * t + k_in) <= q_pos
        key_ref[c] = jnp.where(causal, _sort_key(sc), INT_MIN)
        return carry

    lax.fori_loop(0, nch, score_body, 0)

    def count(pred):
        def body(c, cnt):
            hit = jnp.where(pred(key_ref[c], c), 1, 0)
            return cnt + jnp.sum(hit.reshape(t // SUBLANES, SUBLANES, t), axis=0)
        cnt = lax.fori_loop(0, nch, body, jnp.zeros((SUBLANES, t), I32))
        return jnp.sum(cnt, axis=0, keepdims=True)

    def bit_body(it, carry):
        thr, n_thr = carry
        cand = thr ^ lax.shift_left(jnp.int32(1), 31 - it)
        n_ge = count(lambda key, c: key >= cand)
        ok = n_ge >= k_sel
        return jnp.where(ok, cand, thr), jnp.where(ok, n_ge, n_thr)

    thr, n_thr = lax.fori_loop(0, 32, bit_body, (jnp.full((1, t), INT_MIN, I32),
                                                 jnp.zeros((1, t), I32)))
    thr = jnp.maximum(thr, INT_MIN + 1)
    pfx_ref[...] = jnp.full_like(pfx_ref, 2 ** 30)

    @pl.when(jnp.max(n_thr) > k_sel)
    def _():
        need = k_sel - count(lambda key, c: key > thr)

        def pbit(it, pfx):
            cand = pfx + lax.shift_left(jnp.int32(1), idx_bits - 1 - it)
            n = count(lambda key, c: jnp.logical_and(key == thr, (c * t + k_in) < cand))
            return jnp.where(n < need, cand, pfx)
        pfx_ref[...] = lax.fori_loop(0, idx_bits, pbit, jnp.zeros((1, t), I32))

    pfx = pfx_ref[...]

    def bias_body(c, carry):
        key = key_ref[c]
        tie = jnp.where((c * t + k_in) <= pfx, 0.0, NEG)
        bias_ref[c] = jnp.where(key > thr, 0.0, jnp.where(key == thr, tie, NEG))
        return carry

    lax.fori_loop(0, nch, bias_body, 0)


def _dsa_kernel(q_ref, k_ref, vt_ref, qi_ref, ki_ref, wit_ref, o_ref,
                qm_ref, key_ref, bias_ref, pfx_ref, m_ref, acc_ref, *, k_sel, idx_bits):
    iq = pl.program_id(1)
    t = DSA_T

    @pl.when(pl.program_id(2) == 0)
    def _():
        _dsa_select(iq, qi_ref, ki_ref, wit_ref, qm_ref, key_ref, bias_ref, pfx_ref,
                    k_sel=k_sel, idx_bits=idx_bits)

    _flash_init(m_ref, acc_ref)

    def score(c):
        start = pl.multiple_of(c * t, t)
        bias = bias_ref[c]
        s0 = _dot_nt(k_ref[pl.ds(start, t), 0:LANES], q_ref[:, 0:LANES]) + bias
        s1 = _dot_nt(k_ref[pl.ds(start, t), LANES:2 * LANES], q_ref[:, LANES:2 * LANES]) + bias
        return jnp.concatenate([s0, s1], axis=1)

    _flash_loop(0, iq + 1, score, vt_ref, m_ref, acc_ref)

    o_t = _flash_out(acc_ref)
    row = lax.broadcasted_iota(I32, (LANES, t), 0)
    o_ref[...] = jnp.where(row < C_VDIM, o_t[:, :t], o_t[:, t:]).T.astype(o_ref.dtype)


def _dsa(q, k, vt, qi, ki, wit):
    bsz, seq, _ = q.shape
    t = DSA_T
    nq = seq // t
    k_sel = min(TOPK_MAX, seq // 4)
    idx_bits = max(1, (seq - 1).bit_length())
    kern = functools.partial(_dsa_kernel, k_sel=k_sel, idx_bits=idx_bits)
    return pl.pallas_call(
        kern,
        grid=(bsz, nq, C_HEADS // 2),
        in_specs=[
            pl.BlockSpec((None, t, 2 * LANES), lambda b, i, h: (b, i, h)),
            pl.BlockSpec((None, seq, 2 * LANES), lambda b, i, h: (b, 0, h)),
            pl.BlockSpec((None, None, nq, LANES, t), lambda b, i, h: (b, h, 0, 0, 0)),
            pl.BlockSpec((None, t, IDX_HEADS * IDX_DIM), lambda b, i, h: (b, i, 0)),
            pl.BlockSpec((None, seq, LANES), lambda b, i, h: (b, 0, 0)),
            pl.BlockSpec((None, LANES, t), lambda b, i, h: (b, 0, i)),
        ],
        out_specs=pl.BlockSpec((None, t, LANES), lambda b, i, h: (b, i, h)),
        out_shape=jax.ShapeDtypeStruct((bsz, seq, C_HEADS * C_VDIM), BF16),
        scratch_shapes=[
            pltpu.VMEM((IDX_HEADS, t, LANES), BF16),
            pltpu.VMEM((nq, t, t), I32),
            pltpu.VMEM((nq, t, t), F32),
            pltpu.VMEM((1, t), I32),
            pltpu.VMEM((1, 2 * t), F32),
            pltpu.VMEM((LANES + SUM_ROWS, 2 * t), F32),
        ],
        compiler_params=_cparams(("parallel", "arbitrary", "arbitrary")),
        name="dsa",
    )(q, k, vt, qi, ki, wit)


def _head_tile(nope, rope):
    ref = nope if nope is not None else rope
    zeros = lambda n: jnp.zeros(ref.shape[:-1] + (n,), ref.dtype)
    half = C_ROPE // 2
    split = LANES // 2 - half
    x1, x2 = (zeros(half), zeros(half)) if rope is None else (rope[..., :half], rope[..., half:])
    na, nb = ((zeros(split), zeros(C_NOPE - split)) if nope is None
              else (nope[..., :split], nope[..., split:]))
    return jnp.concatenate([x1, na, x2, nb, zeros(LANES - C_NOPE - C_ROPE)], axis=-1)


def _prep_c_weights(w_in_c, w_uq, w_qi, w_uk, w_uv):
    d = w_in_c.shape[0]
    cq_w, ckv_w, kr_w, kidx_w, widx_w = jnp.split(
        w_in_c, [C_Q_RANK, C_Q_RANK + C_KV_RANK, C_Q_RANK + C_KV_RANK + C_ROPE,
                 C_Q_RANK + C_KV_RANK + C_ROPE + IDX_DIM], axis=1)
    idx_perm = _pair_perm(IDX_DIM, IDX_ROPE // 2)
    w_c = jnp.concatenate([cq_w, ckv_w, _head_tile(None, kr_w),
                           _permute_tiles(jnp.concatenate([kidx_w, kidx_w], axis=1), idx_perm),
                           widx_w, jnp.zeros((d, LANES - IDX_HEADS), w_in_c.dtype)], axis=1)
    uq = w_uq.reshape(C_Q_RANK, C_HEADS, C_NOPE + C_ROPE)
    w_q = _head_tile(uq[..., :C_NOPE], uq[..., C_NOPE:]).reshape(C_Q_RANK, C_HEADS * LANES)
    w_k = _head_tile(w_uk, None).reshape(C_KV_RANK, C_HEADS * LANES)
    w_vt = w_uv.reshape(C_KV_RANK, C_HEADS * C_VDIM).T
    w_qi_p = _permute_tiles(w_qi, idx_perm)
    return (w_c.astype(BF16), w_q.astype(BF16), w_k.astype(BF16), w_vt.astype(BF16),
            w_qi_p.astype(BF16))


def _mixer_ab(h, g_pre, w_in, w_out, g_post, lam_params, g_sub, lam_init, tabs):
    cos_a, sin_a = tabs
    aq, ak, av, bq, bk, bv = jnp.split(w_in, [512, 1024, 1536, 2048, 2560], axis=1)
    w_qk = _permute_tiles(jnp.concatenate([aq, ak, bq, bk], axis=1),
                          _pair_perm(HEAD_DIM, HEAD_DIM // 2)).astype(BF16)
    w_vt = jnp.concatenate([av, bv], axis=1).T.astype(BF16)
    qk, vt = _proj_ab(h, g_pre, w_qk, w_vt, cos_a, sin_a, ATT_T,
                      q_groups=(0, 1, 2, 3, 8, 9, 10, 11), q_scale=HEAD_DIM ** -0.5 * LOG2E)
    gsub_col = g_sub.reshape(A_VDIM, 1)
    ao = _pair_attn(qk, vt, 0, 4, 0, _causal_bias(ATT_T), lam_params, gsub_col,
                    windowed=False, diff=True, lam_init=lam_init)
    bo = _pair_attn(qk, vt, 8, 12, 4, _dilated_bias(ATT_T), lam_params, gsub_col,
                    windowed=True, diff=False, lam_init=lam_init)
    return _outproj_res([ao, bo], w_out.astype(BF16), g_post, h)


def _mixer_c(h, g_pre, w_in_c, g_cq, g_ckv, w_uq, w_qi, w_uk, w_uv, w_out, g_post, tabs):
    cos_h, sin_h, cos_p, sin_p = tabs
    w_c, w_q, w_k, w_vt, w_qi_p = _prep_c_weights(w_in_c, w_uq, w_qi, w_uk, w_uv)
    pc = _proj_c(h, g_pre, w_c)
    q, k, vt, qi, ki, wit = _c_prep(pc, g_cq.reshape(1, -1), g_ckv.reshape(1, -1), w_q, w_k, w_vt,
                                    w_qi_p, cos_h, sin_h, cos_p, sin_p)
    o = _dsa(q, k, vt, qi, ki, wit)
    return _outproj_res([o], w_out.astype(BF16), g_post, h)


def kernel(x, p, g_mix_pre, g_mix_post, g_mlp_pre, g_mlp_post, w_mlp_in, w_mlp_out, w_ple_proj, w_ple_gate, w_in_ab, w_out_ab, diff_lq1, diff_lk1, diff_lq2, diff_lk2, g_diff_sub, w_in_c, g_cq, g_ckv, w_uq, w_qi, w_uk, w_uv, w_out_c):
    bsz, seq, d = x.shape
    depth = p.shape[0]
    tabs_a = _rope_tables(seq, HEAD_DIM // 2, HEAD_DIM // 2)
    tabs_c = (_rope_tables(seq, C_ROPE // 2, LANES // 2)
              + _rope_tables(seq, IDX_ROPE // 2, IDX_DIM // 2))
    h = x
    for i in range(depth):
        j = i // 2
        if i % 2 == 0:
            lam_init = 0.8 - 0.6 * math.exp(-0.3 * i)
            lam_params = jnp.stack([diff_lq1[j], diff_lk1[j], diff_lq2[j], diff_lk2[j]])
            h = _mixer_ab(h, g_mix_pre[i], w_in_ab[j], w_out_ab[j], g_mix_post[i], lam_params,
                          g_diff_sub[j], lam_init, tabs_a)
        else:
            h = _mixer_c(h, g_mix_pre[i], w_in_c[j], g_cq[j], g_ckv[j], w_uq[j], w_qi[j],
                         w_uk[j], w_uv[j], w_out_c[j], g_mix_post[i], tabs_c)
        hf = _mlp_ple(h.reshape(bsz * seq, d), g_mlp_pre[i], w_mlp_in[i].astype(BF16),
                      w_mlp_out[i].astype(BF16), g_mlp_post[i], p[i].reshape(bsz * seq, -1),
                      w_ple_gate[i].astype(BF16), w_ple_proj[i].astype(BF16))
        h = hf.reshape(bsz, seq, d)
    return h
```

```python
import functools
import math

import jax
import jax.numpy as jnp
import numpy as np
from jax import lax
from jax.experimental import pallas as pl
from jax.experimental.pallas import tpu as pltpu

F32 = jnp.float32
BF16 = jnp.bfloat16
I32 = jnp.int32

LANES = 128
SUBLANES = 8
VMEM_LIMIT = 56 * 1024 * 1024

D_MODEL = 1024
HEAD_DIM = 64
A_HEADS = 4
A_VDIM = 2 * HEAD_DIM
B_HEADS = 8
B_BRANCHES = ((128, 1), (512, 4), (2048, 16))
C_HEADS = 16
C_Q_RANK = 384
C_KV_RANK = 256
C_NOPE = 64
C_ROPE = 32
C_VDIM = 64
C_SCALE = (C_NOPE + C_ROPE) ** -0.5
IDX_HEADS = 8
IDX_DIM = 64
IDX_ROPE = 32
TOPK_MAX = 256
D_FF = 4 * D_MODEL
PLE_DIM = 256
ROPE_THETA = 10000.0
EPS = 1e-6
NEG = -1e30
LOG2E = 1.4426950408889634
SUM_ROWS = 16
FLASH_UNROLL = 4
INT_MIN = -2 ** 31

C_PAD = 1024
DSA_T = 512
ATT_T = 512
PAIRS = 4


def _cparams(sem):
    return pltpu.CompilerParams(dimension_semantics=sem, vmem_limit_bytes=VMEM_LIMIT)


def _rms_f32(x, g):
    return x * lax.rsqrt(jnp.mean(x * x, axis=-1, keepdims=True) + EPS) * g


def _dot_nt(a, b):
    return lax.dot_general(a, b, (((1,), (1,)), ((), ())), preferred_element_type=F32)


def _rope_tile(y, cos, sin):
    return y * cos + pltpu.roll(y, LANES // 2, 1) * sin


def _rope_tables(seq, half, group):
    inv = ROPE_THETA ** (-np.arange(half, dtype=np.float64) / half)
    ang = np.arange(seq, dtype=np.float64)[:, None] * inv[None, :]
    lane = np.arange(LANES)
    idx = lane % half
    on = (lane % group < half)[None, :]
    cos = np.where(on, np.cos(ang)[:, idx], 1.0)
    sgn = np.where(lane < LANES // 2, -1.0, 1.0)[None, :]
    sin = np.where(on, np.sin(ang)[:, idx] * sgn, 0.0)
    return cos.astype(np.float32), sin.astype(np.float32)


def _pair_perm(unit, half):
    rest = (unit - 2 * half) // 2
    order = []
    for part in range(2):
        for head in range(2):
            base = head * unit
            order += list(range(base + part * half, base + (part + 1) * half))
            order += list(range(base + 2 * half + part * rest, base + 2 * half + (part + 1) * rest))
    return np.asarray(order, dtype=np.int32)


def _permute_tiles(w, perm):
    rows, cols = w.shape
    return w.reshape(rows, cols // LANES, LANES)[:, :, perm].reshape(rows, cols)


def _proj_ab_kernel(x_ref, g_ref, w_ref, wvt_ref, cos_ref, sin_ref, o_ref, vt_ref, *,
                    q_groups, q_scale):
    xn = _rms_f32(x_ref[...], g_ref[...]).astype(BF16)
    tm = xn.shape[0]
    n = w_ref.shape[1]
    slab = 512
    for s0 in range(0, n, slab):
        y = jnp.dot(xn, w_ref[:, s0:s0 + slab], preferred_element_type=F32)
        for c0 in range(0, slab, LANES):
            yc = _rope_tile(y[:, c0:c0 + LANES], cos_ref[...], sin_ref[...])
            if (s0 + c0) // LANES in q_groups:
                yc = yc * q_scale
            o_ref[:, s0 + c0:s0 + c0 + LANES] = yc.astype(o_ref.dtype)
    tk = vt_ref.shape[-1]
    for hb in range(vt_ref.shape[0]):
        vt = _dot_nt(wvt_ref[hb * LANES:(hb + 1) * LANES, :], xn).astype(vt_ref.dtype)
        for s in range(tm // tk):
            vt_ref[hb, s] = vt[:, s * tk:(s + 1) * tk]


def _proj_ab(x, g, w_qk, w_vt, cos, sin, tk, q_groups, q_scale, tm=512):
    bsz, seq, d = x.shape
    n = w_qk.shape[1]
    nvb = w_vt.shape[0] // LANES
    return pl.pallas_call(
        functools.partial(_proj_ab_kernel, q_groups=frozenset(q_groups), q_scale=q_scale),
        grid=(bsz, seq // tm),
        in_specs=[
            pl.BlockSpec((None, tm, d), lambda b, i: (b, i, 0)),
            pl.BlockSpec((1, d), lambda b, i: (0, 0)),
            pl.BlockSpec((d, n), lambda b, i: (0, 0)),
            pl.BlockSpec(w_vt.shape, lambda b, i: (0, 0)),
            pl.BlockSpec((tm, LANES), lambda b, i: (i, 0)),
            pl.BlockSpec((tm, LANES), lambda b, i: (i, 0)),
        ],
        out_specs=[
            pl.BlockSpec((None, tm, n), lambda b, i: (b, i, 0)),
            pl.BlockSpec((None, nvb, tm // tk, LANES, tk), lambda b, i: (b, 0, i, 0, 0)),
        ],
        out_shape=[
            jax.ShapeDtypeStruct((bsz, seq, n), BF16),
            jax.ShapeDtypeStruct((bsz, nvb, seq // tk, LANES, tk), BF16),
        ],
        compiler_params=_cparams(("parallel", "parallel")),
        name="proj_ab",
    )(x, g.reshape(1, d), w_qk, w_vt, cos, sin)


def _proj_c_kernel(x_ref, g_ref, w_ref, o_ref):
    xn = _rms_f32(x_ref[...], g_ref[...]).astype(BF16)
    o_ref[...] = jnp.dot(xn, w_ref[...], preferred_element_type=F32)


def _proj_c(x, g, w, tm=512):
    bsz, seq, d = x.shape
    n = w.shape[1]
    return pl.pallas_call(
        _proj_c_kernel,
        grid=(bsz, seq // tm),
        in_specs=[
            pl.BlockSpec((None, tm, d), lambda b, i: (b, i, 0)),
            pl.BlockSpec((1, d), lambda b, i: (0, 0)),
            pl.BlockSpec((d, n), lambda b, i: (0, 0)),
        ],
        out_specs=pl.BlockSpec((None, tm, n), lambda b, i: (b, i, 0)),
        out_shape=jax.ShapeDtypeStruct((bsz, seq, n), F32),
        compiler_params=_cparams(("parallel", "parallel")),
        name="proj_c",
    )(x, g.reshape(1, d), w)


def _channel_kernel(*refs, n_in):
    y_refs = refs[:n_in]
    (wo_ref, go_ref, h_ref, g1_ref, w1_ref, w2_ref, g2_ref, p_ref, wg_ref, wp_ref,
     o_ref, hm_ref, xn_ref, acc_ref) = refs[n_in:]
    j = pl.program_id(1)

    @pl.when(j == 0)
    def _():
        y = None
        off = 0
        for y_ref in y_refs:
            k = y_ref.shape[1]
            part = jnp.dot(y_ref[...], wo_ref[off:off + k, :], preferred_element_type=F32)
            y = part if y is None else y + part
            off += k
        hm = h_ref[...] + _rms_f32(y, go_ref[...])
        hm_ref[...] = hm
        xn_ref[...] = _rms_f32(hm, g1_ref[...]).astype(BF16)
        acc_ref[...] = jnp.zeros_like(acc_ref)

    a = jnp.dot(xn_ref[...], w1_ref[...], preferred_element_type=F32)
    a = jnp.square(jnp.maximum(a, 0.0)).astype(BF16)
    acc_ref[...] += jnp.dot(a, w2_ref[...], preferred_element_type=F32)

    @pl.when(j == pl.num_programs(1) - 1)
    def _():
        h = hm_ref[...] + _rms_f32(acc_ref[...], g2_ref[...])
        gate = jax.nn.sigmoid(jnp.dot(h.astype(BF16), wg_ref[...], preferred_element_type=F32))
        emb = jnp.dot(p_ref[...].astype(BF16), wp_ref[...], preferred_element_type=F32)
        o_ref[...] = h + gate * emb


def _channel_block(ys, wo, go, h, g1, w1, w2, g2, p, wg, wp, tm=1024, tf=512):
    t, d = h.shape
    ff = w1.shape[1]
    pd = p.shape[1]
    row = lambda i, j: (i, 0)
    fixed = lambda i, j: (0, 0)
    vec = pl.BlockSpec((1, d), fixed)
    in_specs = [pl.BlockSpec((tm, y.shape[1]), row) for y in ys]
    in_specs += [
        pl.BlockSpec(wo.shape, fixed), vec,
        pl.BlockSpec((tm, d), row), vec,
        pl.BlockSpec((d, tf), lambda i, j: (0, j)),
        pl.BlockSpec((tf, d), lambda i, j: (j, 0)),
        vec,
        pl.BlockSpec((tm, pd), row),
        pl.BlockSpec((d, d), fixed),
        pl.BlockSpec((pd, d), fixed),
    ]
    return pl.pallas_call(
        functools.partial(_channel_kernel, n_in=len(ys)),
        grid=(t // tm, ff // tf),
        in_specs=in_specs,
        out_specs=pl.BlockSpec((tm, d), row),
        out_shape=jax.ShapeDtypeStruct((t, d), F32),
        scratch_shapes=[pltpu.VMEM((tm, d), F32), pltpu.VMEM((tm, d), BF16),
                        pltpu.VMEM((tm, d), F32)],
        compiler_params=_cparams(("parallel", "arbitrary")),
        name="channel_block",
    )(*ys, wo, go.reshape(1, d), h, g1.reshape(1, d), w1, w2, g2.reshape(1, d), p, wg, wp)


def _flash_update(s_t, vt, m_ref, acc_ref):
    m_old = m_ref[...]
    m_new = jnp.maximum(m_old, jnp.max(s_t, axis=0, keepdims=True))
    alpha = jnp.exp2(m_old - m_new)
    p = jnp.exp2(s_t - m_new).astype(BF16)
    vt1 = jnp.concatenate([vt, jnp.ones((SUM_ROWS, vt.shape[1]), vt.dtype)], axis=0)
    acc_ref[...] = alpha * acc_ref[...] + jnp.dot(vt1, p, preferred_element_type=F32)
    m_ref[...] = m_new


def _flash_loop(lo, hi, score, vt_ref, m_ref, acc_ref):
    start = lo
    width = FLASH_UNROLL
    while width >= 1:
        trips = (hi - start) // width

        def body(jj, carry, start=start, width=width):
            j0 = start + width * jj
            scores = [score(j0 + u) for u in range(width)]
            for u in range(width):
                _flash_update(scores[u], vt_ref[j0 + u], m_ref, acc_ref)
            return carry

        lax.fori_loop(0, trips, body, 0)
        start = start + width * trips
        width //= 2


def _flash_init(m_ref, acc_ref):
    m_ref[...] = jnp.full_like(m_ref, NEG)
    acc_ref[...] = jnp.zeros_like(acc_ref)


def _flash_out(acc_ref):
    dv = acc_ref.shape[0] - SUM_ROWS
    return acc_ref[0:dv, :] / acc_ref[dv:dv + 1, :]


def _pair_attn_kernel(q_ref, k_ref, vt_ref, bias_ref, lam_ref, gsub_ref, o_ref,
                      m_ref, acc_ref, *, windowed, diff, lam_init):
    i = pl.program_id(2)
    t = q_ref.shape[0]
    nb = bias_ref.shape[0]
    lane = lax.broadcasted_iota(I32, (t, LANES), 1)
    q = q_ref[...]
    zero = jnp.zeros_like(q)
    first = (lane % HEAD_DIM) < HEAD_DIM // 2
    qs = jnp.concatenate([jnp.where(first, q, zero), jnp.where(first, zero, q)], axis=0)
    _flash_init(m_ref, acc_ref)

    def score(j):
        start = pl.multiple_of(j * t, t)
        bias = bias_ref[jnp.minimum(i - j, nb - 1)]
        return _dot_nt(k_ref[pl.ds(start, t), :], qs) + jnp.concatenate([bias, bias], axis=1)

    lo = jnp.maximum(i - (nb - 1), 0) if windowed else 0
    _flash_loop(lo, i + 1, score, vt_ref, m_ref, acc_ref)

    o_t = _flash_out(acc_ref)
    oa, ob = o_t[:, :t], o_t[:, t:]
    if diff:
        lp = lam_ref[...]
        lam = (jnp.exp(jnp.sum(lp[0:1] * lp[1:2], axis=-1, keepdims=True))
               - jnp.exp(jnp.sum(lp[2:3] * lp[3:4], axis=-1, keepdims=True)) + lam_init)
        d = oa - lam * ob
        y = d * lax.rsqrt(jnp.mean(d * d, axis=0, keepdims=True) + EPS) * gsub_ref[...]
        y = y * (1.0 - lam_init)
    else:
        row = lax.broadcasted_iota(I32, (LANES, t), 0)
        y = jnp.where(row < HEAD_DIM, oa, ob)
    o_ref[...] = y.T.astype(o_ref.dtype)


def _pair_attn(qk, vt, qcol, kcol, vblk, bias, lam_params, gsub, *, windowed, diff, lam_init):
    bsz, seq, _ = qk.shape
    t = ATT_T
    kern = functools.partial(_pair_attn_kernel, windowed=windowed, diff=diff, lam_init=lam_init)
    return pl.pallas_call(
        kern,
        grid=(bsz, PAIRS, seq // t),
        in_specs=[
            pl.BlockSpec((None, t, LANES), lambda b, h, i: (b, i, qcol + h)),
            pl.BlockSpec((None, seq, LANES), lambda b, h, i: (b, 0, kcol + h)),
            pl.BlockSpec((None, None, seq // t, LANES, t), lambda b, h, i: (b, vblk + h, 0, 0, 0)),
            pl.BlockSpec(bias.shape, lambda b, h, i: (0, 0, 0)),
            pl.BlockSpec(lam_params.shape, lambda b, h, i: (0, 0)),
            pl.BlockSpec(gsub.shape, lambda b, h, i: (0, 0)),
        ],
        out_specs=pl.BlockSpec((None, t, LANES), lambda b, h, i: (b, i, h)),
        out_shape=jax.ShapeDtypeStruct((bsz, seq, PAIRS * LANES), BF16),
        scratch_shapes=[pltpu.VMEM((1, 2 * t), F32),
                        pltpu.VMEM((LANES + SUM_ROWS, 2 * t), F32)],
        compiler_params=_cparams(("parallel", "parallel", "arbitrary")),
        name="diff_attn" if diff else "dilated_attn",
    )(qk, qk, vt, bias, lam_params, gsub)


def _causal_bias(t):
    r = np.arange(t)
    tri = np.where(r[None, :] >= r[:, None], 0.0, NEG).astype(np.float32)
    return np.stack([tri, np.zeros((t, t), np.float32)])


def _dilated_bias(t):
    max_dist = max(w for w, _ in B_BRANCHES)
    n_off = max_dist // t + 1
    r = np.arange(t)
    tiles = []
    for o in range(n_off):
        dist = o * t + r[None, :] - r[:, None]
        cnt = np.zeros((t, t), np.float64)
        for w, d in B_BRANCHES:
            cnt = cnt + ((dist >= 0) & (dist <= w) & (dist % d == 0))
        tiles.append(np.where(cnt > 0, np.log2(np.maximum(cnt, 1.0)), NEG))
    return np.stack(tiles).astype(np.float32)


def _c_prep_kernel(pc_ref, gq_ref, gkv_ref, wq_ref, wk_ref, wvt_ref, wqi_ref,
                   ch_ref, sh_ref, cp_ref, sp_ref,
                   q_ref, k_ref, vt_ref, qi_ref, ki_ref, wit_ref):
    cqn = _rms_f32(pc_ref[:, 0:C_Q_RANK], gq_ref[...]).astype(BF16)
    c0 = C_Q_RANK
    ckvn = _rms_f32(pc_ref[:, c0:c0 + C_KV_RANK], gkv_ref[...]).astype(BF16)
    c0 += C_KV_RANK
    kr = _rope_tile(pc_ref[:, c0:c0 + LANES], ch_ref[...], sh_ref[...])
    c0 += LANES
    ki_ref[...] = _rope_tile(pc_ref[:, c0:c0 + LANES], cp_ref[...], sp_ref[...]).astype(BF16)
    c0 += LANES
    wit_ref[...] = (pc_ref[:, c0:c0 + LANES] * (IDX_HEADS ** -0.5)).T

    qi = jnp.dot(cqn, wqi_ref[...], preferred_element_type=F32)
    for c in range(0, IDX_HEADS * IDX_DIM, LANES):
        qi_ref[:, c:c + LANES] = _rope_tile(qi[:, c:c + LANES], cp_ref[...],
                                            sp_ref[...]).astype(BF16)

    for hp in range(C_HEADS // 2):
        pair = slice(2 * hp * LANES, 2 * (hp + 1) * LANES)
        q2 = jnp.dot(cqn, wq_ref[:, pair], preferred_element_type=F32)
        k2 = jnp.dot(ckvn, wk_ref[:, pair], preferred_element_type=F32)
        for e in range(2):
            cols = slice((2 * hp + e) * LANES, (2 * hp + e + 1) * LANES)
            qh = q2[:, e * LANES:(e + 1) * LANES]
            q_ref[:, cols] = (_rope_tile(qh, ch_ref[...], sh_ref[...])
                              * (C_SCALE * LOG2E)).astype(BF16)
            k_ref[:, cols] = (k2[:, e * LANES:(e + 1) * LANES] + kr).astype(BF16)
    for hp in range(C_HEADS // 2):
        vt_ref[hp] = _dot_nt(wvt_ref[hp * LANES:(hp + 1) * LANES, :], ckvn).astype(BF16)


def _c_prep(pc, gq, gkv, wq, wk, wvt, wqi, ch, sh, cp, sp):
    bsz, seq, _ = pc.shape
    tm = DSA_T
    hw = C_HEADS * LANES
    row = lambda b, i: (b, i, 0)
    tab = pl.BlockSpec((tm, LANES), lambda b, i: (i, 0))
    full2 = lambda a: pl.BlockSpec(a.shape, lambda b, i: (0, 0))
    return pl.pallas_call(
        _c_prep_kernel,
        grid=(bsz, seq // tm),
        in_specs=[
            pl.BlockSpec((None, tm, C_PAD), row),
            full2(gq), full2(gkv), full2(wq), full2(wk), full2(wvt), full2(wqi),
            tab, tab, tab, tab,
        ],
        out_specs=[
            pl.BlockSpec((None, tm, hw), row),
            pl.BlockSpec((None, tm, hw), row),
            pl.BlockSpec((None, C_HEADS // 2, None, LANES, tm), lambda b, i: (b, 0, i, 0, 0)),
            pl.BlockSpec((None, tm, IDX_HEADS * IDX_DIM), row),
            pl.BlockSpec((None, tm, LANES), row),
            pl.BlockSpec((None, LANES, tm), lambda b, i: (b, 0, i)),
        ],
        out_shape=[
            jax.ShapeDtypeStruct((bsz, seq, hw), BF16),
            jax.ShapeDtypeStruct((bsz, seq, hw), BF16),
            jax.ShapeDtypeStruct((bsz, C_HEADS // 2, seq // tm, LANES, tm), BF16),
            jax.ShapeDtypeStruct((bsz, seq, IDX_HEADS * IDX_DIM), BF16),
            jax.ShapeDtypeStruct((bsz, seq, LANES), BF16),
            jax.ShapeDtypeStruct((bsz, LANES, seq), F32),
        ],
        compiler_params=_cparams(("parallel", "parallel")),
        name="c_prep",
    )(pc, gq, gkv, wq, wk, wvt, wqi, ch, sh, cp, sp)


def _sort_key(x):
    b = pltpu.bitcast(x, I32)
    return b ^ (lax.shift_right_arithmetic(b, 31) & 0x7FFFFFFF)


def _dsa_select(iq, qi_ref, ki_ref, wit_ref, qm_ref, key_ref, bias_ref, pfx_ref, *,
                k_sel, idx_bits):
    t = DSA_T
    nch = iq + 1
    q_pos = iq * t + lax.broadcasted_iota(I32, (t, t), 1)
    k_in = lax.broadcasted_iota(I32, (t, t), 0)

    lane = lax.broadcasted_iota(I32, (t, LANES), 1)
    for h in range(IDX_HEADS):
        tile = qi_ref[:, (h // 2) * LANES:(h // 2 + 1) * LANES]
        even = (lane % IDX_DIM) < IDX_DIM // 2
        keep = even if h % 2 == 0 else jnp.logical_not(even)
        qm_ref[h] = jnp.where(keep, tile, jnp.zeros_like(tile))

    def score_body(c, carry):
        start = pl.multiple_of(c * t, t)
        kic = ki_ref[pl.ds(start, t), :]
        sc = jnp.zeros((t, t), F32)
        for h in range(IDX_HEADS):
            sc = sc + jnp.maximum(_dot_nt(kic, qm_ref[h]), 0.0) * wit_ref[h:h + 1, :]
        sc = sc * (IDX_DIM ** -0.5)
        causal = (c * t + k_in) <= q_pos
        key_ref[c] = jnp.where(causal, _sort_key(sc), INT_MIN)
        return carry

    lax.fori_loop(0, nch, score_body, 0)

    def count(pred):
        def body(c, cnt):
            hit = jnp.where(pred(key_ref[c], c), 1, 0)
            return cnt + jnp.sum(hit.reshape(t // SUBLANES, SUBLANES, t), axis=0)
        cnt = lax.fori_loop(0, nch, body, jnp.zeros((SUBLANES, t), I32))
        return jnp.sum(cnt, axis=0, keepdims=True)

    def bit_body(it, carry):
        thr, n_thr = carry
        cand = thr ^ lax.shift_left(jnp.int32(1), 31 - it)
        n_ge = count(lambda key, c: key >= cand)
        ok = n_ge >= k_sel
        return jnp.where(ok, cand, thr), jnp.where(ok, n_ge, n_thr)

    thr, n_thr = lax.fori_loop(0, 32, bit_body, (jnp.full((1, t), INT_MIN, I32),
                                                 jnp.zeros((1, t), I32)))
    thr = jnp.maximum(thr, INT_MIN + 1)
    pfx_ref[...] = jnp.full_like(pfx_ref, 2 ** 30)

    @pl.when(jnp.max(n_thr) > k_sel)
    def _():
        need = k_sel - count(lambda key, c: key > thr)

        def pbit(it, pfx):
            cand = pfx + lax.shift_left(jnp.int32(1), idx_bits - 1 - it)
            n = count(lambda key, c: jnp.logical_and(key == thr, (c * t + k_in) < cand))
            return jnp.where(n < need, cand, pfx)
        pfx_ref[...] = lax.fori_loop(0, idx_bits, pbit, jnp.zeros((1, t), I32))

    pfx = pfx_ref[...]

    def bias_body(c, carry):
        key = key_ref[c]
        tie = jnp.where((c * t + k_in) <= pfx, 0.0, NEG)
        bias_ref[c] = jnp.where(key > thr, 0.0, jnp.where(key == thr, tie, NEG))
        return carry

    lax.fori_loop(0, nch, bias_body, 0)


def _dsa_kernel(q_ref, k_ref, vt_ref, qi_ref, ki_ref, wit_ref, o_ref,
                qm_ref, key_ref, bias_ref, pfx_ref, m_ref, acc_ref, *, k_sel, idx_bits):
    iq = pl.program_id(1)
    t = DSA_T

    @pl.when(pl.program_id(2) == 0)
    def _():
        _dsa_select(iq, qi_ref, ki_ref, wit_ref, qm_ref, key_ref, bias_ref, pfx_ref,
                    k_sel=k_sel, idx_bits=idx_bits)

    _flash_init(m_ref, acc_ref)

    def score(c):
        start = pl.multiple_of(c * t, t)
        bias = bias_ref[c]
        s0 = _dot_nt(k_ref[pl.ds(start, t), 0:LANES], q_ref[:, 0:LANES]) + bias
        s1 = _dot_nt(k_ref[pl.ds(start, t), LANES:2 * LANES], q_ref[:, LANES:2 * LANES]) + bias
        return jnp.concatenate([s0, s1], axis=1)

    _flash_loop(0, iq + 1, score, vt_ref, m_ref, acc_ref)

    o_t = _flash_out(acc_ref)
    row = lax.broadcasted_iota(I32, (LANES, t), 0)
    o_ref[...] = jnp.where(row < C_VDIM, o_t[:, :t], o_t[:, t:]).T.astype(o_ref.dtype)


def _dsa(q, k, vt, qi, ki, wit):
    bsz, seq, _ = q.shape
    t = DSA_T
    nq = seq // t
    k_sel = min(TOPK_MAX, seq // 4)
    idx_bits = max(1, (seq - 1).bit_length())
    kern = functools.partial(_dsa_kernel, k_sel=k_sel, idx_bits=idx_bits)
    return pl.pallas_call(
        kern,
        grid=(bsz, nq, C_HEADS // 2),
        in_specs=[
            pl.BlockSpec((None, t, 2 * LANES), lambda b, i, h: (b, i, h)),
            pl.BlockSpec((None, seq, 2 * LANES), lambda b, i, h: (b, 0, h)),
            pl.BlockSpec((None, None, nq, LANES, t), lambda b, i, h: (b, h, 0, 0, 0)),
            pl.BlockSpec((None, t, IDX_HEADS * IDX_DIM), lambda b, i, h: (b, i, 0)),
            pl.BlockSpec((None, seq, LANES), lambda b, i, h: (b, 0, 0)),
            pl.BlockSpec((None, LANES, t), lambda b, i, h: (b, 0, i)),
        ],
        out_specs=pl.BlockSpec((None, t, LANES), lambda b, i, h: (b, i, h)),
        out_shape=jax.ShapeDtypeStruct((bsz, seq, C_HEADS * C_VDIM), BF16),
        scratch_shapes=[
            pltpu.VMEM((IDX_HEADS, t, LANES), BF16),
            pltpu.VMEM((nq, t, t), I32),
            pltpu.VMEM((nq, t, t), F32),
            pltpu.VMEM((1, t), I32),
            pltpu.VMEM((1, 2 * t), F32),
            pltpu.VMEM((LANES + SUM_ROWS, 2 * t), F32),
        ],
        compiler_params=_cparams(("parallel", "arbitrary", "arbitrary")),
        name="dsa",
    )(q, k, vt, qi, ki, wit)


def _head_tile(nope, rope):
    ref = nope if nope is not None else rope
    zeros = lambda n: jnp.zeros(ref.shape[:-1] + (n,), ref.dtype)
    half = C_ROPE // 2
    split = LANES // 2 - half
    x1, x2 = (zeros(half), zeros(half)) if rope is None else (rope[..., :half], rope[..., half:])
    na, nb = ((zeros(split), zeros(C_NOPE - split)) if nope is None
              else (nope[..., :split], nope[..., split:]))
    return jnp.concatenate([x1, na, x2, nb, zeros(LANES - C_NOPE - C_ROPE)], axis=-1)


def _prep_c_weights(w_in_c, w_uq, w_qi, w_uk, w_uv):
    d = w_in_c.shape[0]
    cq_w, ckv_w, kr_w, kidx_w, widx_w = jnp.split(
        w_in_c, [C_Q_RANK, C_Q_RANK + C_KV_RANK, C_Q_RANK + C_KV_RANK + C_ROPE,
                 C_Q_RANK + C_KV_RANK + C_ROPE + IDX_DIM], axis=1)
    idx_perm = _pair_perm(IDX_DIM, IDX_ROPE // 2)
    w_c = jnp.concatenate([cq_w, ckv_w, _head_tile(None, kr_w),
                           _permute_tiles(jnp.concatenate([kidx_w, kidx_w], axis=1), idx_perm),
                           widx_w, jnp.zeros((d, LANES - IDX_HEADS), w_in_c.dtype)], axis=1)
    uq = w_uq.reshape(C_Q_RANK, C_HEADS, C_NOPE + C_ROPE)
    w_q = _head_tile(uq[..., :C_NOPE], uq[..., C_NOPE:]).reshape(C_Q_RANK, C_HEADS * LANES)
    w_k = _head_tile(w_uk, None).reshape(C_KV_RANK, C_HEADS * LANES)
    w_vt = w_uv.reshape(C_KV_RANK, C_HEADS * C_VDIM).T
    w_qi_p = _permute_tiles(w_qi, idx_perm)
    return (w_c.astype(BF16), w_q.astype(BF16), w_k.astype(BF16), w_vt.astype(BF16),
            w_qi_p.astype(BF16))


def _mixer_ab(h, g_pre, w_in, lam_params, g_sub, lam_init, tabs):
    cos_a, sin_a = tabs
    aq, ak, av, bq, bk, bv = jnp.split(w_in, [512, 1024, 1536, 2048, 2560], axis=1)
    w_qk = _permute_tiles(jnp.concatenate([aq, ak, bq, bk], axis=1),
                          _pair_perm(HEAD_DIM, HEAD_DIM // 2)).astype(BF16)
    w_vt = jnp.concatenate([av, bv], axis=1).T.astype(BF16)
    qk, vt = _proj_ab(h, g_pre, w_qk, w_vt, cos_a, sin_a, ATT_T,
                      q_groups=(0, 1, 2, 3, 8, 9, 10, 11), q_scale=HEAD_DIM ** -0.5 * LOG2E)
    gsub_col = g_sub.reshape(A_VDIM, 1)
    ao = _pair_attn(qk, vt, 0, 4, 0, _causal_bias(ATT_T), lam_params, gsub_col,
                    windowed=False, diff=True, lam_init=lam_init)
    bo = _pair_attn(qk, vt, 8, 12, 4, _dilated_bias(ATT_T), lam_params, gsub_col,
                    windowed=True, diff=False, lam_init=lam_init)
    return [ao, bo]


def _mixer_c(h, g_pre, w_in_c, g_cq, g_ckv, w_uq, w_qi, w_uk, w_uv, tabs):
    cos_h, sin_h, cos_p, sin_p = tabs
    w_c, w_q, w_k, w_vt, w_qi_p = _prep_c_weights(w_in_c, w_uq, w_qi, w_uk, w_uv)
    pc = _proj_c(h, g_pre, w_c)
    q, k, vt, qi, ki, wit = _c_prep(pc, g_cq.reshape(1, -1), g_ckv.reshape(1, -1), w_q, w_k, w_vt,
                                    w_qi_p, cos_h, sin_h, cos_p, sin_p)
    return [_dsa(q, k, vt, qi, ki, wit)]


def kernel(x, p, g_mix_pre, g_mix_post, g_mlp_pre, g_mlp_post, w_mlp_in, w_mlp_out, w_ple_proj, w_ple_gate, w_in_ab, w_out_ab, diff_lq1, diff_lk1, diff_lq2, diff_lk2, g_diff_sub, w_in_c, g_cq, g_ckv, w_uq, w_qi, w_uk, w_uv, w_out_c):
    bsz, seq, d = x.shape
    depth = p.shape[0]
    tabs_a = _rope_tables(seq, HEAD_DIM // 2, HEAD_DIM // 2)
    tabs_c = (_rope_tables(seq, C_ROPE // 2, LANES // 2)
              + _rope_tables(seq, IDX_ROPE // 2, IDX_DIM // 2))
    h = x
    for i in range(depth):
        j = i // 2
        if i % 2 == 0:
            lam_init = 0.8 - 0.6 * math.exp(-0.3 * i)
            lam_params = jnp.stack([diff_lq1[j], diff_lk1[j], diff_lq2[j], diff_lk2[j]])
            ys = _mixer_ab(h, g_mix_pre[i], w_in_ab[j], lam_params, g_diff_sub[j], lam_init, tabs_a)
            w_out = w_out_ab[j]
        else:
            ys = _mixer_c(h, g_mix_pre[i], w_in_c[j], g_cq[j], g_ckv[j], w_uq[j], w_qi[j],
                          w_uk[j], w_uv[j], tabs_c)
            w_out = w_out_c[j]
        hf = _channel_block([y.reshape(bsz * seq, -1) for y in ys], w_out.astype(BF16),
                            g_mix_post[i], h.reshape(bsz * seq, d), g_mlp_pre[i],
                            w_mlp_in[i].astype(BF16), w_mlp_out[i].astype(BF16), g_mlp_post[i],
                            p[i].reshape(bsz * seq, -1), w_ple_gate[i].astype(BF16),
                            w_ple_proj[i].astype(BF16))
        h = hf.reshape(bsz, seq, d)
    return h
```

```python
import functools
import math

import jax
import jax.numpy as jnp
import numpy as np
from jax import lax
from jax.experimental import pallas as pl
from jax.experimental.pallas import tpu as pltpu

F32 = jnp.float32
BF16 = jnp.bfloat16
I32 = jnp.int32

LANES = 128
SUBLANES = 8
VMEM_LIMIT = 56 * 1024 * 1024

D_MODEL = 1024
HEAD_DIM = 64
A_HEADS = 4
A_VDIM = 2 * HEAD_DIM
B_HEADS = 8
B_BRANCHES = ((128, 1), (512, 4), (2048, 16))
C_HEADS = 16
C_Q_RANK = 384
C_KV_RANK = 256
C_NOPE = 64
C_ROPE = 32
C_VDIM = 64
C_SCALE = (C_NOPE + C_ROPE) ** -0.5
IDX_HEADS = 8
IDX_DIM = 64
IDX_ROPE = 32
TOPK_MAX = 256
D_FF = 4 * D_MODEL
PLE_DIM = 256
ROPE_THETA = 10000.0
EPS = 1e-6
NEG = -1e30
LOG2E = 1.4426950408889634
SUM_ROWS = 16
FLASH_UNROLL = 4
INT_MIN = -2 ** 31
FLT_MAX = float(np.finfo(np.float32).max)

C_PAD = 1024
DSA_T = 512
ATT_T = 512
PAIRS = 4


def _cparams(sem):
    return pltpu.CompilerParams(dimension_semantics=sem, vmem_limit_bytes=VMEM_LIMIT)


def _rms_f32(x, g):
    return x * lax.rsqrt(jnp.mean(x * x, axis=-1, keepdims=True) + EPS) * g


def _dot_nt(a, b):
    return lax.dot_general(a, b, (((1,), (1,)), ((), ())), preferred_element_type=F32)


def _rope_tile(y, cos, sin):
    return y * cos + pltpu.roll(y, LANES // 2, 1) * sin


def _rope_tables(seq, half, group):
    inv = ROPE_THETA ** (-np.arange(half, dtype=np.float64) / half)
    ang = np.arange(seq, dtype=np.float64)[:, None] * inv[None, :]
    lane = np.arange(LANES)
    idx = lane % half
    on = (lane % group < half)[None, :]
    cos = np.where(on, np.cos(ang)[:, idx], 1.0)
    sgn = np.where(lane < LANES // 2, -1.0, 1.0)[None, :]
    sin = np.where(on, np.sin(ang)[:, idx] * sgn, 0.0)
    return cos.astype(np.float32), sin.astype(np.float32)


def _pair_perm(unit, half):
    rest = (unit - 2 * half) // 2
    order = []
    for part in range(2):
        for head in range(2):
            base = head * unit
            order += list(range(base + part * half, base + (part + 1) * half))
            order += list(range(base + 2 * half + part * rest, base + 2 * half + (part + 1) * rest))
    return np.asarray(order, dtype=np.int32)


def _permute_tiles(w, perm):
    rows, cols = w.shape
    return w.reshape(rows, cols // LANES, LANES)[:, :, perm].reshape(rows, cols)


def _proj_ab_kernel(x_ref, g_ref, w_ref, wvt_ref, cos_ref, sin_ref, o_ref, vt_ref, *,
                    q_groups, q_scale):
    xn = _rms_f32(x_ref[...], g_ref[...]).astype(BF16)
    tm = xn.shape[0]
    n = w_ref.shape[1]
    slab = 512
    for s0 in range(0, n, slab):
        y = jnp.dot(xn, w_ref[:, s0:s0 + slab], preferred_element_type=F32)
        for c0 in range(0, slab, LANES):
            yc = _rope_tile(y[:, c0:c0 + LANES], cos_ref[...], sin_ref[...])
            if (s0 + c0) // LANES in q_groups:
                yc = yc * q_scale
            o_ref[:, s0 + c0:s0 + c0 + LANES] = yc.astype(o_ref.dtype)
    tk = vt_ref.shape[-1]
    for hb in range(vt_ref.shape[0]):
        vt = _dot_nt(wvt_ref[hb * LANES:(hb + 1) * LANES, :], xn).astype(vt_ref.dtype)
        for s in range(tm // tk):
            vt_ref[hb, s] = vt[:, s * tk:(s + 1) * tk]


def _proj_ab(x, g, w_qk, w_vt, cos, sin, tk, q_groups, q_scale, tm=512):
    bsz, seq, d = x.shape
    n = w_qk.shape[1]
    nvb = w_vt.shape[0] // LANES
    return pl.pallas_call(
        functools.partial(_proj_ab_kernel, q_groups=frozenset(q_groups), q_scale=q_scale),
        grid=(bsz, seq // tm),
        in_specs=[
            pl.BlockSpec((None, tm, d), lambda b, i: (b, i, 0)),
            pl.BlockSpec((1, d), lambda b, i: (0, 0)),
            pl.BlockSpec((d, n), lambda b, i: (0, 0)),
            pl.BlockSpec(w_vt.shape, lambda b, i: (0, 0)),
            pl.BlockSpec((tm, LANES), lambda b, i: (i, 0)),
            pl.BlockSpec((tm, LANES), lambda b, i: (i, 0)),
        ],
        out_specs=[
            pl.BlockSpec((None, tm, n), lambda b, i: (b, i, 0)),
            pl.BlockSpec((None, nvb, tm // tk, LANES, tk), lambda b, i: (b, 0, i, 0, 0)),
        ],
        out_shape=[
            jax.ShapeDtypeStruct((bsz, seq, n), BF16),
            jax.ShapeDtypeStruct((bsz, nvb, seq // tk, LANES, tk), BF16),
        ],
        compiler_params=_cparams(("parallel", "parallel")),
        name="proj_ab",
    )(x, g.reshape(1, d), w_qk, w_vt, cos, sin)


def _channel_kernel(*refs, n_in):
    y_refs = refs[:n_in]
    (wo_ref, go_ref, h_ref, g1_ref, w1_ref, w2_ref, g2_ref, p_ref, wg_ref, wp_ref,
     o_ref, hm_ref, xn_ref, acc_ref) = refs[n_in:]
    j = pl.program_id(1)

    @pl.when(j == 0)
    def _():
        y = None
        off = 0
        for y_ref in y_refs:
            k = y_ref.shape[1]
            part = jnp.dot(y_ref[...], wo_ref[off:off + k, :], preferred_element_type=F32)
            y = part if y is None else y + part
            off += k
        hm = h_ref[...] + _rms_f32(y, go_ref[...])
        hm_ref[...] = hm
        xn_ref[...] = _rms_f32(hm, g1_ref[...]).astype(BF16)
        acc_ref[...] = jnp.zeros_like(acc_ref)

    a = jnp.dot(xn_ref[...], w1_ref[...], preferred_element_type=F32)
    a = jnp.square(jnp.maximum(a, 0.0)).astype(BF16)
    acc_ref[...] += jnp.dot(a, w2_ref[...], preferred_element_type=F32)

    @pl.when(j == pl.num_programs(1) - 1)
    def _():
        h = hm_ref[...] + _rms_f32(acc_ref[...], g2_ref[...])
        gate = jax.nn.sigmoid(jnp.dot(h.astype(BF16), wg_ref[...], preferred_element_type=F32))
        emb = jnp.dot(p_ref[...].astype(BF16), wp_ref[...], preferred_element_type=F32)
        o_ref[...] = h + gate * emb


def _channel_block(ys, wo, go, h, g1, w1, w2, g2, p, wg, wp, tm=1024, tf=512):
    t, d = h.shape
    ff = w1.shape[1]
    pd = p.shape[1]
    row = lambda i, j: (i, 0)
    fixed = lambda i, j: (0, 0)
    vec = pl.BlockSpec((1, d), fixed)
    in_specs = [pl.BlockSpec((tm, y.shape[1]), row) for y in ys]
    in_specs += [
        pl.BlockSpec(wo.shape, fixed), vec,
        pl.BlockSpec((tm, d), row), vec,
        pl.BlockSpec((d, tf), lambda i, j: (0, j)),
        pl.BlockSpec((tf, d), lambda i, j: (j, 0)),
        vec,
        pl.BlockSpec((tm, pd), row),
        pl.BlockSpec((d, d), fixed),
        pl.BlockSpec((pd, d), fixed),
    ]
    return pl.pallas_call(
        functools.partial(_channel_kernel, n_in=len(ys)),
        grid=(t // tm, ff // tf),
        in_specs=in_specs,
        out_specs=pl.BlockSpec((tm, d), row),
        out_shape=jax.ShapeDtypeStruct((t, d), F32),
        scratch_shapes=[pltpu.VMEM((tm, d), F32), pltpu.VMEM((tm, d), BF16),
                        pltpu.VMEM((tm, d), F32)],
        compiler_params=_cparams(("parallel", "arbitrary")),
        name="channel_block",
    )(*ys, wo, go.reshape(1, d), h, g1.reshape(1, d), w1, w2, g2.reshape(1, d), p, wg, wp)


def _flash_update(s_t, vt, m_ref, acc_ref):
    m_old = m_ref[...]
    m_new = jnp.maximum(m_old, jnp.max(s_t, axis=0, keepdims=True))
    alpha = jnp.exp2(m_old - m_new)
    p = jnp.exp2(s_t - m_new).astype(BF16)
    vt1 = jnp.concatenate([vt, jnp.ones((SUM_ROWS, vt.shape[1]), vt.dtype)], axis=0)
    acc_ref[...] = alpha * acc_ref[...] + jnp.dot(vt1, p, preferred_element_type=F32)
    m_ref[...] = m_new


def _flash_loop(lo, hi, score, vt_ref, m_ref, acc_ref):
    s_t = score(lo)
    m_first = jnp.max(s_t, axis=0, keepdims=True)
    vt = vt_ref[lo]
    vt1 = jnp.concatenate([vt, jnp.ones((SUM_ROWS, vt.shape[1]), vt.dtype)], axis=0)
    acc_ref[...] = jnp.dot(vt1, jnp.exp2(s_t - m_first).astype(BF16), preferred_element_type=F32)
    m_ref[...] = m_first

    start = lo + 1
    width = FLASH_UNROLL
    while width >= 1:
        trips = (hi - start) // width

        def body(jj, carry, start=start, width=width):
            j0 = start + width * jj
            scores = [score(j0 + u) for u in range(width)]
            for u in range(width):
                _flash_update(scores[u], vt_ref[j0 + u], m_ref, acc_ref)
            return carry

        lax.fori_loop(0, trips, body, 0)
        start = start + width * trips
        width //= 2


def _flash_out(acc_ref):
    dv = acc_ref.shape[0] - SUM_ROWS
    return acc_ref[0:dv, :] / acc_ref[dv:dv + 1, :]


def _pair_attn_kernel(q_ref, k_ref, vt_ref, bias_ref, lam_ref, gsub_ref, o_ref,
                      m_ref, acc_ref, *, windowed, diff, lam_init):
    i = pl.program_id(2)
    t = q_ref.shape[0]
    nb = bias_ref.shape[0]
    lane = lax.broadcasted_iota(I32, (t, LANES), 1)
    q = q_ref[...]
    zero = jnp.zeros_like(q)
    first = (lane % HEAD_DIM) < HEAD_DIM // 2
    qs = jnp.concatenate([jnp.where(first, q, zero), jnp.where(first, zero, q)], axis=0)

    def score(j):
        start = pl.multiple_of(j * t, t)
        bias = bias_ref[jnp.minimum(i - j, nb - 1)]
        return _dot_nt(k_ref[pl.ds(start, t), :], qs) + jnp.concatenate([bias, bias], axis=1)

    lo = jnp.maximum(i - (nb - 1), 0) if windowed else 0
    _flash_loop(lo, i + 1, score, vt_ref, m_ref, acc_ref)

    o_t = _flash_out(acc_ref)
    oa, ob = o_t[:, :t], o_t[:, t:]
    if diff:
        lp = lam_ref[...]
        lam = (jnp.exp(jnp.sum(lp[0:1] * lp[1:2], axis=-1, keepdims=True))
               - jnp.exp(jnp.sum(lp[2:3] * lp[3:4], axis=-1, keepdims=True)) + lam_init)
        d = oa - lam * ob
        y = d * lax.rsqrt(jnp.mean(d * d, axis=0, keepdims=True) + EPS) * gsub_ref[...]
        y = y * (1.0 - lam_init)
    else:
        row = lax.broadcasted_iota(I32, (LANES, t), 0)
        y = jnp.where(row < HEAD_DIM, oa, ob)
    o_ref[...] = y.T.astype(o_ref.dtype)


def _pair_attn(qk, vt, qcol, kcol, vblk, bias, lam_params, gsub, *, windowed, diff, lam_init):
    bsz, seq, _ = qk.shape
    t = ATT_T
    kern = functools.partial(_pair_attn_kernel, windowed=windowed, diff=diff, lam_init=lam_init)
    return pl.pallas_call(
        kern,
        grid=(bsz, PAIRS, seq // t),
        in_specs=[
            pl.BlockSpec((None, t, LANES), lambda b, h, i: (b, i, qcol + h)),
            pl.BlockSpec((None, seq, LANES), lambda b, h, i: (b, 0, kcol + h)),
            pl.BlockSpec((None, None, seq // t, LANES, t), lambda b, h, i: (b, vblk + h, 0, 0, 0)),
            pl.BlockSpec(bias.shape, lambda b, h, i: (0, 0, 0)),
            pl.BlockSpec(lam_params.shape, lambda b, h, i: (0, 0)),
            pl.BlockSpec(gsub.shape, lambda b, h, i: (0, 0)),
        ],
        out_specs=pl.BlockSpec((None, t, LANES), lambda b, h, i: (b, i, h)),
        out_shape=jax.ShapeDtypeStruct((bsz, seq, PAIRS * LANES), BF16),
        scratch_shapes=[pltpu.VMEM((1, 2 * t), F32),
                        pltpu.VMEM((LANES + SUM_ROWS, 2 * t), F32)],
        compiler_params=_cparams(("parallel", "parallel", "arbitrary")),
        name="diff_attn" if diff else "dilated_attn",
    )(qk, qk, vt, bias, lam_params, gsub)


def _causal_bias(t):
    r = np.arange(t)
    tri = np.where(r[None, :] >= r[:, None], 0.0, NEG).astype(np.float32)
    return np.stack([tri, np.zeros((t, t), np.float32)])


def _dilated_bias(t):
    max_dist = max(w for w, _ in B_BRANCHES)
    n_off = max_dist // t + 1
    r = np.arange(t)
    tiles = []
    for o in range(n_off):
        dist = o * t + r[None, :] - r[:, None]
        cnt = np.zeros((t, t), np.float64)
        for w, d in B_BRANCHES:
            cnt = cnt + ((dist >= 0) & (dist <= w) & (dist % d == 0))
        tiles.append(np.where(cnt > 0, np.log2(np.maximum(cnt, 1.0)), NEG))
    return np.stack(tiles).astype(np.float32)


def _c_prep_kernel(x_ref, g_ref, wc_ref, gq_ref, gkv_ref, wq_ref, wk_ref, wvt_ref, wqi_ref,
                   ch_ref, sh_ref, cp_ref, sp_ref,
                   q_ref, k_ref, vt_ref, qi_ref, ki_ref, wit_ref):
    xn = _rms_f32(x_ref[...], g_ref[...]).astype(BF16)
    pc = jnp.dot(xn, wc_ref[...], preferred_element_type=F32)
    cqn = _rms_f32(pc[:, 0:C_Q_RANK], gq_ref[...]).astype(BF16)
    c0 = C_Q_RANK
    ckvn = _rms_f32(pc[:, c0:c0 + C_KV_RANK], gkv_ref[...]).astype(BF16)
    c0 += C_KV_RANK
    kr = _rope_tile(pc[:, c0:c0 + LANES], ch_ref[...], sh_ref[...])
    c0 += LANES
    ki_ref[...] = _rope_tile(pc[:, c0:c0 + LANES], cp_ref[...], sp_ref[...]).astype(BF16)
    c0 += LANES
    wit_ref[...] = (pc[:, c0:c0 + LANES] * (IDX_HEADS ** -0.5)).T

    qi = jnp.dot(cqn, wqi_ref[...], preferred_element_type=F32)
    for c in range(0, IDX_HEADS * IDX_DIM, LANES):
        qi_ref[:, c:c + LANES] = _rope_tile(qi[:, c:c + LANES], cp_ref[...],
                                            sp_ref[...]).astype(BF16)

    for hp in range(C_HEADS // 2):
        pair = slice(2 * hp * LANES, 2 * (hp + 1) * LANES)
        q2 = jnp.dot(cqn, wq_ref[:, pair], preferred_element_type=F32)
        k2 = jnp.dot(ckvn, wk_ref[:, pair], preferred_element_type=F32)
        for e in range(2):
            cols = slice((2 * hp + e) * LANES, (2 * hp + e + 1) * LANES)
            qh = q2[:, e * LANES:(e + 1) * LANES]
            q_ref[:, cols] = (_rope_tile(qh, ch_ref[...], sh_ref[...])
                              * (C_SCALE * LOG2E)).astype(BF16)
            k_ref[:, cols] = (k2[:, e * LANES:(e + 1) * LANES] + kr).astype(BF16)
    for hp in range(C_HEADS // 2):
        vt_ref[hp] = _dot_nt(wvt_ref[hp * LANES:(hp + 1) * LANES, :], ckvn).astype(BF16)


def _c_prep(x, g, wc, gq, gkv, wq, wk, wvt, wqi, ch, sh, cp, sp):
    bsz, seq, d = x.shape
    tm = DSA_T
    hw = C_HEADS * LANES
    row = lambda b, i: (b, i, 0)
    tab = pl.BlockSpec((tm, LANES), lambda b, i: (i, 0))
    full2 = lambda a: pl.BlockSpec(a.shape, lambda b, i: (0, 0))
    return pl.pallas_call(
        _c_prep_kernel,
        grid=(bsz, seq // tm),
        in_specs=[
            pl.BlockSpec((None, tm, d), row),
            full2(g), full2(wc),
            full2(gq), full2(gkv), full2(wq), full2(wk), full2(wvt), full2(wqi),
            tab, tab, tab, tab,
        ],
        out_specs=[
            pl.BlockSpec((None, tm, hw), row),
            pl.BlockSpec((None, tm, hw), row),
            pl.BlockSpec((None, C_HEADS // 2, None, LANES, tm), lambda b, i: (b, 0, i, 0, 0)),
            pl.BlockSpec((None, tm, IDX_HEADS * IDX_DIM), row),
            pl.BlockSpec((None, tm, LANES), row),
            pl.BlockSpec((None, LANES, tm), lambda b, i: (b, 0, i)),
        ],
        out_shape=[
            jax.ShapeDtypeStruct((bsz, seq, hw), BF16),
            jax.ShapeDtypeStruct((bsz, seq, hw), BF16),
            jax.ShapeDtypeStruct((bsz, C_HEADS // 2, seq // tm, LANES, tm), BF16),
            jax.ShapeDtypeStruct((bsz, seq, IDX_HEADS * IDX_DIM), BF16),
            jax.ShapeDtypeStruct((bsz, seq, LANES), BF16),
            jax.ShapeDtypeStruct((bsz, LANES, seq), F32),
        ],
        compiler_params=_cparams(("parallel", "parallel")),
        name="c_prep",
    )(x, g, wc, gq, gkv, wq, wk, wvt, wqi, ch, sh, cp, sp)


def _code_value(code):
    bits = code ^ (lax.shift_right_arithmetic(code, 31) & 0x7FFFFFFF)
    return pltpu.bitcast(bits, F32)


def _dsa_select(iq, qi_ref, ki_ref, wit_ref, qm_ref, key_ref, bias_ref, pfx_ref, *,
                k_sel, idx_bits):
    t = DSA_T
    nch = iq + 1
    q_pos = iq * t + lax.broadcasted_iota(I32, (t, t), 1)
    k_in = lax.broadcasted_iota(I32, (t, t), 0)

    lane = lax.broadcasted_iota(I32, (t, LANES), 1)
    for h in range(IDX_HEADS):
        tile = qi_ref[:, (h // 2) * LANES:(h // 2 + 1) * LANES]
        even = (lane % IDX_DIM) < IDX_DIM // 2
        keep = even if h % 2 == 0 else jnp.logical_not(even)
        qm_ref[h] = jnp.where(keep, tile, jnp.zeros_like(tile))

    def score_body(c, carry):
        start = pl.multiple_of(c * t, t)
        kic = ki_ref[pl.ds(start, t), :]
        sc = jnp.zeros((t, t), F32)
        for h in range(IDX_HEADS):
            sc = sc + jnp.maximum(_dot_nt(kic, qm_ref[h]), 0.0) * wit_ref[h:h + 1, :]
        sc = sc * (IDX_DIM ** -0.5)
        causal = (c * t + k_in) <= q_pos
        key_ref[c] = jnp.where(causal, sc, -jnp.inf)
        return carry

    lax.fori_loop(0, nch, score_body, 0)

    def count(pred):
        def body(c, cnt):
            hit = jnp.where(pred(key_ref[c], c), 1, 0)
            return cnt + jnp.sum(hit.reshape(t // SUBLANES, SUBLANES, t), axis=0)
        cnt = lax.fori_loop(0, nch, body, jnp.zeros((SUBLANES, t), I32))
        return jnp.sum(cnt, axis=0, keepdims=True)

    def bit_body(it, carry):
        code, n_thr = carry
        cand = code ^ lax.shift_left(jnp.int32(1), 31 - it)
        cand_f = _code_value(cand)
        n_ge = count(lambda sc, c: sc >= cand_f)
        ok = n_ge >= k_sel
        return jnp.where(ok, cand, code), jnp.where(ok, n_ge, n_thr)

    code, n_thr = lax.fori_loop(0, 32, bit_body, (jnp.full((1, t), INT_MIN, I32),
                                                  jnp.zeros((1, t), I32)))
    thr = jnp.where(n_thr > 0, _code_value(code), -FLT_MAX)
    pfx_ref[...] = jnp.full_like(pfx_ref, 2 ** 30)

    @pl.when(jnp.max(n_thr) > k_sel)
    def _():
        need = k_sel - count(lambda sc, c: sc > thr)

        def pbit(it, pfx):
            cand = pfx + lax.shift_left(jnp.int32(1), idx_bits - 1 - it)
            n = count(lambda sc, c: jnp.logical_and(sc == thr, (c * t + k_in) < cand))
            return jnp.where(n < need, cand, pfx)
        pfx_ref[...] = lax.fori_loop(0, idx_bits, pbit, jnp.zeros((1, t), I32))

    pfx = pfx_ref[...]

    def bias_body(c, carry):
        sc = key_ref[c]
        tie = jnp.where((c * t + k_in) <= pfx, 0.0, NEG)
        bias_ref[c] = jnp.where(sc > thr, 0.0, jnp.where(sc == thr, tie, NEG))
        return carry

    lax.fori_loop(0, nch, bias_body, 0)


def _dsa_kernel(q_ref, k_ref, vt_ref, qi_ref, ki_ref, wit_ref, o_ref,
                qm_ref, key_ref, bias_ref, pfx_ref, m_ref, acc_ref, *, k_sel, idx_bits):
    iq = pl.program_id(1)
    t = DSA_T

    @pl.when(pl.program_id(2) == 0)
    def _():
        _dsa_select(iq, qi_ref, ki_ref, wit_ref, qm_ref, key_ref, bias_ref, pfx_ref,
                    k_sel=k_sel, idx_bits=idx_bits)


    def score(c):
        start = pl.multiple_of(c * t, t)
        bias = bias_ref[c]
        s0 = _dot_nt(k_ref[pl.ds(start, t), 0:LANES], q_ref[:, 0:LANES]) + bias
        s1 = _dot_nt(k_ref[pl.ds(start, t), LANES:2 * LANES], q_ref[:, LANES:2 * LANES]) + bias
        return jnp.concatenate([s0, s1], axis=1)

    _flash_loop(0, iq + 1, score, vt_ref, m_ref, acc_ref)

    o_t = _flash_out(acc_ref)
    row = lax.broadcasted_iota(I32, (LANES, t), 0)
    o_ref[...] = jnp.where(row < C_VDIM, o_t[:, :t], o_t[:, t:]).T.astype(o_ref.dtype)


def _dsa(q, k, vt, qi, ki, wit):
    bsz, seq, _ = q.shape
    t = DSA_T
    nq = seq // t
    k_sel = min(TOPK_MAX, seq // 4)
    idx_bits = max(1, (seq - 1).bit_length())
    kern = functools.partial(_dsa_kernel, k_sel=k_sel, idx_bits=idx_bits)
    return pl.pallas_call(
        kern,
        grid=(bsz, nq, C_HEADS // 2),
        in_specs=[
            pl.BlockSpec((None, t, 2 * LANES), lambda b, i, h: (b, i, h)),
            pl.BlockSpec((None, seq, 2 * LANES), lambda b, i, h: (b, 0, h)),
            pl.BlockSpec((None, None, nq, LANES, t), lambda b, i, h: (b, h, 0, 0, 0)),
            pl.BlockSpec((None, t, IDX_HEADS * IDX_DIM), lambda b, i, h: (b, i, 0)),
            pl.BlockSpec((None, seq, LANES), lambda b, i, h: (b, 0, 0)),
            pl.BlockSpec((None, LANES, t), lambda b, i, h: (b, 0, i)),
        ],
        out_specs=pl.BlockSpec((None, t, LANES), lambda b, i, h: (b, i, h)),
        out_shape=jax.ShapeDtypeStruct((bsz, seq, C_HEADS * C_VDIM), BF16),
        scratch_shapes=[
            pltpu.VMEM((IDX_HEADS, t, LANES), BF16),
            pltpu.VMEM((nq, t, t), F32),
            pltpu.VMEM((nq, t, t), F32),
            pltpu.VMEM((1, t), I32),
            pltpu.VMEM((1, 2 * t), F32),
            pltpu.VMEM((LANES + SUM_ROWS, 2 * t), F32),
        ],
        compiler_params=_cparams(("parallel", "arbitrary", "arbitrary")),
        name="dsa",
    )(q, k, vt, qi, ki, wit)


def _head_tile(nope, rope):
    ref = nope if nope is not None else rope
    zeros = lambda n: jnp.zeros(ref.shape[:-1] + (n,), ref.dtype)
    half = C_ROPE // 2
    split = LANES // 2 - half
    x1, x2 = (zeros(half), zeros(half)) if rope is None else (rope[..., :half], rope[..., half:])
    na, nb = ((zeros(split), zeros(C_NOPE - split)) if nope is None
              else (nope[..., :split], nope[..., split:]))
    return jnp.concatenate([x1, na, x2, nb, zeros(LANES - C_NOPE - C_ROPE)], axis=-1)


def _prep_c_weights(w_in_c, w_uq, w_qi, w_uk, w_uv):
    d = w_in_c.shape[0]
    cq_w, ckv_w, kr_w, kidx_w, widx_w = jnp.split(
        w_in_c, [C_Q_RANK, C_Q_RANK + C_KV_RANK, C_Q_RANK + C_KV_RANK + C_ROPE,
                 C_Q_RANK + C_KV_RANK + C_ROPE + IDX_DIM], axis=1)
    idx_perm = _pair_perm(IDX_DIM, IDX_ROPE // 2)
    w_c = jnp.concatenate([cq_w, ckv_w, _head_tile(None, kr_w),
                           _permute_tiles(jnp.concatenate([kidx_w, kidx_w], axis=1), idx_perm),
                           widx_w, jnp.zeros((d, LANES - IDX_HEADS), w_in_c.dtype)], axis=1)
    uq = w_uq.reshape(C_Q_RANK, C_HEADS, C_NOPE + C_ROPE)
    w_q = _head_tile(uq[..., :C_NOPE], uq[..., C_NOPE:]).reshape(C_Q_RANK, C_HEADS * LANES)
    w_k = _head_tile(w_uk, None).reshape(C_KV_RANK, C_HEADS * LANES)
    w_vt = w_uv.reshape(C_KV_RANK, C_HEADS * C_VDIM).T
    w_qi_p = _permute_tiles(w_qi, idx_perm)
    return (w_c.astype(BF16), w_q.astype(BF16), w_k.astype(BF16), w_vt.astype(BF16),
            w_qi_p.astype(BF16))


def _mixer_ab(h, g_pre, w_in, lam_params, g_sub, lam_init, tabs):
    cos_a, sin_a = tabs
    aq, ak, av, bq, bk, bv = jnp.split(w_in, [512, 1024, 1536, 2048, 2560], axis=1)
    w_qk = _permute_tiles(jnp.concatenate([aq, ak, bq, bk], axis=1),
                          _pair_perm(HEAD_DIM, HEAD_DIM // 2)).astype(BF16)
    w_vt = jnp.concatenate([av, bv], axis=1).T.astype(BF16)
    qk, vt = _proj_ab(h, g_pre, w_qk, w_vt, cos_a, sin_a, ATT_T,
                      q_groups=(0, 1, 2, 3, 8, 9, 10, 11), q_scale=HEAD_DIM ** -0.5 * LOG2E)
    gsub_col = g_sub.reshape(A_VDIM, 1)
    ao = _pair_attn(qk, vt, 0, 4, 0, _causal_bias(ATT_T), lam_params, gsub_col,
                    windowed=False, diff=True, lam_init=lam_init)
    bo = _pair_attn(qk, vt, 8, 12, 4, _dilated_bias(ATT_T), lam_params, gsub_col,
                    windowed=True, diff=False, lam_init=lam_init)
    return [ao, bo]


def _mixer_c(h, g_pre, w_in_c, g_cq, g_ckv, w_uq, w_qi, w_uk, w_uv, tabs):
    cos_h, sin_h, cos_p, sin_p = tabs
    w_c, w_q, w_k, w_vt, w_qi_p = _prep_c_weights(w_in_c, w_uq, w_qi, w_uk, w_uv)
    q, k, vt, qi, ki, wit = _c_prep(h, g_pre.reshape(1, -1), w_c, g_cq.reshape(1, -1),
                                    g_ckv.reshape(1, -1), w_q, w_k, w_vt, w_qi_p,
                                    cos_h, sin_h, cos_p, sin_p)
    return [_dsa(q, k, vt, qi, ki, wit)]


def kernel(x, p, g_mix_pre, g_mix_post, g_mlp_pre, g_mlp_post, w_mlp_in, w_mlp_out, w_ple_proj, w_ple_gate, w_in_ab, w_out_ab, diff_lq1, diff_lk1, diff_lq2, diff_lk2, g_diff_sub, w_in_c, g_cq, g_ckv, w_uq, w_qi, w_uk, w_uv, w_out_c):
    bsz, seq, d = x.shape
    depth = p.shape[0]
    tabs_a = _rope_tables(seq, HEAD_DIM // 2, HEAD_DIM // 2)
    tabs_c = (_rope_tables(seq, C_ROPE // 2, LANES // 2)
              + _rope_tables(seq, IDX_ROPE // 2, IDX_DIM // 2))
    h = x
    for i in range(depth):
        j = i // 2
        if i % 2 == 0:
            lam_init = 0.8 - 0.6 * math.exp(-0.3 * i)
            lam_params = jnp.stack([diff_lq1[j], diff_lk1[j], diff_lq2[j], diff_lk2[j]])
            ys = _mixer_ab(h, g_mix_pre[i], w_in_ab[j], lam_params, g_diff_sub[j], lam_init, tabs_a)
            w_out = w_out_ab[j]
        else:
            ys = _mixer_c(h, g_mix_pre[i], w_in_c[j], g_cq[j], g_ckv[j], w_uq[j], w_qi[j],
                          w_uk[j], w_uv[j], tabs_c)
            w_out = w_out_c[j]
        hf = _channel_block([y.reshape(bsz * seq, -1) for y in ys], w_out.astype(BF16),
                            g_mix_post[i], h.reshape(bsz * seq, d), g_mlp_pre[i],
                            w_mlp_in[i].astype(BF16), w_mlp_out[i].astype(BF16), g_mlp_post[i],
                            p[i].reshape(bsz * seq, -1), w_ple_gate[i].astype(BF16),
                            w_ple_proj[i].astype(BF16))
        h = hf.reshape(bsz, seq, d)
    return h
```
